```python
import math
import jax, jax.numpy as jnp
from jax import lax
import numpy as np

D_MODEL = 1024
BATCH = 1
SEQ = 16384
DEPTH = 2
DEC_BATCH = 128
DEC_SEQ = 1
PAST_LEN = 16384
PAGE_SIZE = 128

F32 = jnp.float32
ROPE_THETA = 500000.0
EPS = 1e-6
QBLK = 128

GDN_HEADS = 4
GDN_DK = 128
GDN_DV = 128
GDN_CONV = 4
GDN_CHUNK = 64
GDN_CONV_CH = GDN_HEADS * (2 * GDN_DK + GDN_DV)

NSA_HEADS = 8
NSA_KV_HEADS = 2
NSA_GROUP = NSA_HEADS // NSA_KV_HEADS
NSA_DH = 64
NSA_ROT = NSA_DH // 4
NSA_BLK = 64
NSA_TOPN = 16
NSA_LOCAL = 2
NSA_WINDOW = 512
NSA_FORCE = 1.0e4
NSA_SCALE = NSA_DH ** -0.5

SGU_GROUPS = 4
SGU_DG = 128
SGU_CHUNK = 128
SGU_WIDTH = SGU_GROUPS * SGU_DG

MLA_HEADS = 8
MLA_Q_LORA = 256
MLA_KV_LORA = 256
MLA_NOPE = 64
MLA_ROPE = 32
MLA_V = 64
MLA_SCALE = (MLA_NOPE + MLA_ROPE) ** -0.5

D_FF = 4 * D_MODEL

EVEN_SIZES = (GDN_CONV_CH, GDN_HEADS * GDN_DV, GDN_HEADS, GDN_HEADS,
              NSA_HEADS * NSA_DH, 3 * 2 * NSA_KV_HEADS * NSA_DH, 3 * NSA_HEADS)
EVEN_COLS = sum(EVEN_SIZES)
EVEN_OUT = GDN_HEADS * GDN_DV + NSA_HEADS * NSA_DH
ODD_SIZES = (2 * SGU_WIDTH, MLA_Q_LORA, MLA_KV_LORA, MLA_ROPE)
ODD_COLS = sum(ODD_SIZES)
ODD_OUT = SGU_WIDTH + MLA_HEADS * MLA_V

kernel_name = 'hybrid_gdn_nsa_sgu_mla_step'


def split_cols(x, sizes):
    return jnp.split(x, np.cumsum(sizes)[:-1].tolist(), axis=-1)


def rms_norm(x, g):
    xf = x.astype(F32)
    y = xf * lax.rsqrt(jnp.mean(xf * xf, axis=-1, keepdims=True) + EPS)
    return (y * g.astype(F32)).astype(x.dtype)


def layer_norm(x, g, b):
    xf = x.astype(F32)
    xc = xf - jnp.mean(xf, axis=-1, keepdims=True)
    y = xc * lax.rsqrt(jnp.mean(xc * xc, axis=-1, keepdims=True) + EPS)
    return (y * g.astype(F32) + b.astype(F32)).astype(x.dtype)


def l2_normalize(x):
    xf = x.astype(F32)
    return xf * lax.rsqrt(jnp.sum(xf * xf, axis=-1, keepdims=True) + EPS)


def rope(x, pos):
    d = x.shape[-1]
    inv = ROPE_THETA ** (-jnp.arange(0, d, 2, dtype=F32) / d)
    ang = pos.astype(F32)[..., None] * inv
    cos, sin = jnp.cos(ang), jnp.sin(ang)
    xf = x.astype(F32)
    x1, x2 = xf[..., : d // 2], xf[..., d // 2:]
    return jnp.concatenate([x1 * cos - x2 * sin, x2 * cos + x1 * sin], axis=-1).astype(x.dtype)


def partial_rope(x, pos, rot):
    return jnp.concatenate([rope(x[..., :rot], pos), x[..., rot:]], axis=-1)


def masked_softmax(s, mask):
    s = jnp.where(mask, s.astype(F32), -1e30)
    return jax.nn.softmax(s, axis=-1) * jnp.any(mask, axis=-1, keepdims=True)


def causal_conv(x_ext, w):
    t = x_ext.shape[1] - (GDN_CONV - 1)
    return sum(x_ext[:, i:i + t] * w[i] for i in range(GDN_CONV))


def sq_relu_mlp(h, g, w_up, w_down):
    z = rms_norm(h, g) @ w_up
    return h + jnp.square(jax.nn.relu(z)) @ w_down


def gdn_prepare(conv_out, b_raw, a_raw, a_log, dt_bias):
    n, t, _ = conv_out.shape
    q, k, v = split_cols(conv_out, (GDN_HEADS * GDN_DK, GDN_HEADS * GDN_DK, GDN_HEADS * GDN_DV))
    q = l2_normalize(q.reshape(n, t, GDN_HEADS, GDN_DK)) * (GDN_DK ** -0.5)
    k = l2_normalize(k.reshape(n, t, GDN_HEADS, GDN_DK))
    v = v.reshape(n, t, GDN_HEADS, GDN_DV).astype(F32)
    beta = jax.nn.sigmoid(b_raw.astype(F32))
    g = -jnp.exp(a_log.astype(F32)) * jax.nn.softplus(a_raw.astype(F32) + dt_bias.astype(F32))
    return q, k, v, beta, g


def gdn_chunked(q, k, v, beta, g):
    n, t, h, _ = q.shape
    c = GDN_CHUNK
    nc = t // c

    def chunks(x):
        return jnp.moveaxis(x.reshape((n, nc, c, h) + x.shape[3:]), 3, 1)

    q, k, v, beta, g = map(chunks, (q, k, v, beta, g))
    gc = jnp.cumsum(g, axis=-1)
    tril = jnp.tril(jnp.ones((c, c), bool))
    stril = jnp.tril(jnp.ones((c, c), bool), -1)
    diff = gc[..., :, None] - gc[..., None, :]
    decay = jnp.where(tril, jnp.exp(jnp.where(tril, diff, 0.0)), 0.0)
    kb = k * beta[..., None]
    m = jnp.where(stril, jnp.einsum('nhcid,nhcjd->nhcij', kb, k) * decay, 0.0)
    eye = jnp.eye(c, dtype=F32)
    tinv = lax.linalg.triangular_solve(eye + m, jnp.broadcast_to(eye, m.shape),
                                       left_side=True, lower=True, unit_diagonal=True)
    u = tinv @ (v * beta[..., None])
    w = tinv @ (kb * jnp.exp(gc)[..., None])
    attn = jnp.where(tril, jnp.einsum('nhcid,nhcjd->nhcij', q, k) * decay, 0.0)
    qg = q * jnp.exp(gc)[..., None]
    kd = k * jnp.exp(gc[..., -1:] - gc)[..., None]
    gl = jnp.exp(gc[..., -1])

    def step(s, xs):
        u_c, w_c, a_c, qg_c, kd_c, gl_c = xs
        v_new = u_c - jnp.einsum('nhcd,nhde->nhce', w_c, s)
        o_c = jnp.einsum('nhcd,nhde->nhce', qg_c, s) + jnp.einsum('nhij,nhje->nhie', a_c, v_new)
        s = s * gl_c[..., None, None] + jnp.einsum('nhcd,nhce->nhde', kd_c, v_new)
        return s, o_c

    xs = tuple(jnp.moveaxis(x, 2, 0) for x in (u, w, attn, qg, kd, gl))
    s0 = jnp.zeros((n, h, GDN_DK, GDN_DV), F32)
    s_fin, o = lax.scan(step, s0, xs)
    o = jnp.transpose(o, (1, 0, 3, 2, 4)).reshape(n, t, h, GDN_DV)
    return o, s_fin


def gdn_recurrent(q, k, v, beta, g, s0):
    def step(s, xs):
        q_t, k_t, v_t, b_t, g_t = xs
        s = s * jnp.exp(g_t)[..., None, None]
        delta = (v_t - jnp.einsum('nhde,nhd->nhe', s, k_t)) * b_t[..., None]
        s = s + jnp.einsum('nhd,nhe->nhde', k_t, delta)
        return s, jnp.einsum('nhde,nhd->nhe', s, q_t)

    xs = tuple(jnp.moveaxis(x, 1, 0) for x in (q, k, v, beta, g))
    s_fin, o = lax.scan(step, s0, xs)
    return jnp.moveaxis(o, 0, 1), s_fin


def gdn_output(o, z, out_norm):
    n, t = z.shape[:2]
    gate = jax.nn.silu(z.reshape(n, t, GDN_HEADS, GDN_DV).astype(F32))
    return (rms_norm(o, out_norm) * gate).astype(z.dtype).reshape(n, t, GDN_HEADS * GDN_DV)


def nsa_core(q, qpos, ckv, fetch, n_cand, wkv, wpos, gates):
    n, hk, gr, nq, _ = q.shape
    nbc = ckv.shape[3]
    s_c = jnp.einsum('nhgqd,nhjd->nhgqj', q, ckv[:, 0]) * NSA_SCALE
    cmask = (jnp.arange(nbc)[None, :] + 1) * NSA_BLK <= qpos[:, None] + 1
    p_c = masked_softmax(s_c, cmask)
    o_c = jnp.einsum('nhgqj,nhjd->nhgqd', p_c, ckv[:, 1])
    imp = jnp.pad(p_c.sum(axis=2), ((0, 0), (0, 0), (0, 0), (0, n_cand - nbc)))
    j = jnp.arange(n_cand)[None, :]
    cur = (qpos // NSA_BLK)[:, None]
    forced = (j == 0) | ((j > cur - NSA_LOCAL) & (j <= cur))
    score = jnp.where(j <= cur, jnp.where(forced, NSA_FORCE, imp), -jnp.inf)
    _, idx = lax.top_k(score, min(NSA_TOPN, n_cand))
    n_top = idx.shape[-1]
    sel = fetch(idx)
    kpos = idx[..., None] * NSA_BLK + jnp.arange(NSA_BLK)
    smask = (kpos <= qpos[:, None, None]).reshape(n, hk, 1, nq, n_top * NSA_BLK)
    s_s = jnp.einsum('nhgqd,nhqjkd->nhgqjk', q, sel[..., 0, :, :]) * NSA_SCALE
    p_s = masked_softmax(s_s.reshape(n, hk, gr, nq, n_top * NSA_BLK), smask)
    o_s = jnp.einsum('nhgqjk,nhqjkd->nhgqd', p_s.reshape(n, hk, gr, nq, n_top, NSA_BLK), sel[..., 1, :, :])
    s_w = jnp.einsum('nhgqd,nhkd->nhgqk', q, wkv[:, 0]) * NSA_SCALE
    dpos = qpos[:, None] - wpos[None, :]
    wmask = (dpos >= 0) & (dpos <= NSA_WINDOW) & (wpos >= 0)[None, :]
    o_w = jnp.einsum('nhgqk,nhkd->nhgqd', masked_softmax(s_w, wmask), wkv[:, 1])
    g = gates.astype(F32)
    return (g[..., 0:1] * o_c + g[..., 1:2] * o_s + g[..., 2:3] * o_w).astype(q.dtype)


def nsa_prompt(q, kv, gates):
    n, _, _, t, _ = q.shape
    nb = t // NSA_BLK
    ckv = kv[:, 0].reshape(n, 2, NSA_KV_HEADS, nb, NSA_BLK, NSA_DH).mean(axis=4, dtype=F32).astype(q.dtype)
    kvb = kv[:, 1].reshape(n, 2, NSA_KV_HEADS, nb, NSA_BLK, NSA_DH)

    def fetch(idx):
        bi = jnp.arange(n)[:, None, None, None, None]
        hi = jnp.arange(NSA_KV_HEADS)[None, :, None, None, None]
        return kvb[bi, jnp.arange(2), hi, idx[..., None]]

    wpad = jnp.pad(kv[:, 2], ((0, 0), (0, 0), (0, 0), (NSA_WINDOW, 0), (0, 0)))
    nqb = t // QBLK
    qb = jnp.moveaxis(q.reshape(n, NSA_KV_HEADS, NSA_GROUP, nqb, QBLK, NSA_DH), 3, 0)
    gb = jnp.moveaxis(gates.reshape(n, NSA_KV_HEADS, NSA_GROUP, nqb, QBLK, 3), 3, 0)

    def body(args):
        q_blk, g_blk, s0 = args
        qpos = s0 + jnp.arange(QBLK, dtype=jnp.int32)
        wkv = lax.dynamic_slice_in_dim(wpad, s0, NSA_WINDOW + QBLK, axis=3)
        wpos = s0 - NSA_WINDOW + jnp.arange(NSA_WINDOW + QBLK, dtype=jnp.int32)
        return nsa_core(q_blk, qpos, ckv, fetch, nb, wkv, wpos, g_blk)

    o = lax.map(body, (qb, gb, jnp.arange(nqb, dtype=jnp.int32) * QBLK))
    o = jnp.transpose(o, (1, 0, 4, 2, 3, 5)).reshape(n, t, NSA_HEADS * NSA_DH)
    return o, kv[:, 2, :, :, t - min(NSA_WINDOW, t):]


def nsa_sample(q, kv, gates, cache_cmp, cache_sel, win_state, page_table, pos):
    n, _, _, s, _ = q.shape
    n_pool, npg = cache_cmp.shape[0], page_table.shape[1]
    bpp = PAGE_SIZE // NSA_BLK
    nbp = npg * bpp
    past_len = npg * PAGE_SIZE
    shp = (n_pool, 2, NSA_KV_HEADS, bpp, NSA_BLK, NSA_DH)
    means = cache_cmp.reshape(shp).mean(axis=4, dtype=F32).astype(q.dtype)[page_table]
    ck_past = jnp.transpose(means, (0, 2, 3, 1, 4, 5)).reshape(n, 2, NSA_KV_HEADS, nbp, NSA_DH)
    nfull = s // NSA_BLK
    ck_new = kv[:, 0, :, :, :nfull * NSA_BLK].reshape(n, 2, NSA_KV_HEADS, nfull, NSA_BLK, NSA_DH)
    ckv = jnp.concatenate([ck_past, ck_new.mean(axis=4, dtype=F32).astype(q.dtype)], axis=3)
    nnew = -(-s // NSA_BLK)
    newb = jnp.pad(kv[:, 1], ((0, 0), (0, 0), (0, 0), (0, nnew * NSA_BLK - s), (0, 0)))
    newb = newb.reshape(n, 2, NSA_KV_HEADS, nnew, NSA_BLK, NSA_DH)
    pool = cache_sel.reshape(shp)

    def fetch(idx):
        bi = jnp.arange(n)[:, None, None, None]
        hi = jnp.arange(NSA_KV_HEADS)[None, :, None, None]
        kvi = jnp.arange(2)
        phys = page_table[bi, jnp.clip(idx // bpp, 0, npg - 1)]
        past = pool[phys[..., None], kvi, hi[..., None], (idx % bpp)[..., None]]
        new = newb[bi[..., None], kvi, hi[..., None], jnp.clip(idx - nbp, 0, nnew - 1)[..., None]]
        return jnp.where((idx < nbp)[..., None, None, None], past, new)

    wb = win_state.shape[3]
    wkv = jnp.concatenate([win_state.astype(kv.dtype), kv[:, 2]], axis=3)
    wpos = past_len - wb + jnp.arange(wb + s, dtype=jnp.int32)
    o = nsa_core(q, pos, ckv, fetch, nbp + nnew, wkv, wpos, gates)
    o = jnp.transpose(o, (0, 3, 1, 2, 4)).reshape(n, s, NSA_HEADS * NSA_DH)
    return o, wkv[:, :, :, wb + s - min(NSA_WINDOW, past_len + s):]


def even_project(xn, pos, w_in, q_norm, k_norm):
    n, t, _ = xn.shape
    qkv, z, b_raw, a_raw, q, kv, g_raw = split_cols(xn @ w_in, EVEN_SIZES)
    q = partial_rope(rms_norm(q.reshape(n, t, NSA_HEADS, NSA_DH), q_norm), pos[:, None], NSA_ROT)
    q = jnp.transpose(q.reshape(n, t, NSA_KV_HEADS, NSA_GROUP, NSA_DH), (0, 2, 3, 1, 4))
    kv = kv.reshape(n, t, 3, 2, NSA_KV_HEADS, NSA_DH)
    k = partial_rope(rms_norm(kv[:, :, :, 0], k_norm[:, None, :]), pos[:, None, None], NSA_ROT)
    kv = jnp.transpose(jnp.stack([k, kv[:, :, :, 1]], axis=3), (0, 2, 3, 4, 1, 5))
    gates = jax.nn.sigmoid(g_raw.astype(F32)).reshape(n, t, NSA_KV_HEADS, NSA_GROUP, 3)
    return qkv, z, b_raw, a_raw, q, kv, jnp.transpose(gates, (0, 2, 3, 1, 4))


def even_prompt(xn, pos, w_in, w_out, conv_w, a_log, dt_bias, out_norm, q_norm, k_norm):
    n = xn.shape[0]
    qkv, z, b_raw, a_raw, q, kv, gates = even_project(xn, pos, w_in, q_norm, k_norm)
    qkv_ext = jnp.concatenate([jnp.zeros((n, GDN_CONV - 1, GDN_CONV_CH), qkv.dtype), qkv], axis=1)
    gq, gk, gv, beta, g = gdn_prepare(jax.nn.silu(causal_conv(qkv_ext, conv_w)), b_raw, a_raw, a_log, dt_bias)
    o, s_fin = gdn_chunked(gq, gk, gv, beta, g)
    nsa_out, win = nsa_prompt(q, kv, gates)
    y = jnp.concatenate([gdn_output(o, z, out_norm), nsa_out], axis=-1) @ w_out
    return y, (s_fin.astype(xn.dtype), qkv_ext[:, -(GDN_CONV - 1):], kv[:, 0], kv[:, 1], win)


def even_sample(xn, pos, gdn_state, conv_state, cache_cmp, cache_sel, win_state, page_table,
                w_in, w_out, conv_w, a_log, dt_bias, out_norm, q_norm, k_norm):
    qkv, z, b_raw, a_raw, q, kv, gates = even_project(xn, pos, w_in, q_norm, k_norm)
    qkv_ext = jnp.concatenate([conv_state.astype(qkv.dtype), qkv], axis=1)
    gq, gk, gv, beta, g = gdn_prepare(jax.nn.silu(causal_conv(qkv_ext, conv_w)), b_raw, a_raw, a_log, dt_bias)
    o, s_fin = gdn_recurrent(gq, gk, gv, beta, g, gdn_state.astype(F32))
    nsa_out, win = nsa_sample(q, kv, gates, cache_cmp, cache_sel, win_state, page_table, pos)
    y = jnp.concatenate([gdn_output(o, z, out_norm), nsa_out], axis=-1) @ w_out
    return y, (s_fin.astype(xn.dtype), qkv_ext[:, -(GDN_CONV - 1):], kv[:, 0], kv[:, 1], win)


def sgu_mix(u, v, w_s, b_s):
    n, t, _ = u.shape
    l = min(t, SGU_CHUNK)
    nc = t // l
    w = jnp.tril(w_s[:, :l, :l])
    mix = jnp.einsum('gts,ncsgd->nctgd', w, v.reshape(n, nc, l, SGU_GROUPS, SGU_DG))
    mix = mix + jnp.transpose(b_s[:, :l])[:, :, None]
    return (u.reshape(n, nc, l, SGU_GROUPS, SGU_DG) * mix).reshape(n, t, SGU_WIDTH)


def odd_project(xn, pos, w_in, ln_g, ln_b, cq_norm, ckv_norm, w_uq, w_uk, qn_norm, qr_norm, kr_norm):
    n, t, _ = xn.shape
    uv, cq, ckv, kr = split_cols(xn @ w_in, ODD_SIZES)
    uv = jax.nn.gelu(uv, approximate=False)
    u = uv[..., :SGU_WIDTH]
    v = layer_norm(uv[..., SGU_WIDTH:], ln_g, ln_b)
    q = (rms_norm(cq, cq_norm) @ w_uq).reshape(n, t, MLA_HEADS, MLA_NOPE + MLA_ROPE)
    q_nope = rms_norm(q[..., :MLA_NOPE], qn_norm)
    q_pe = rope(rms_norm(q[..., MLA_NOPE:], qr_norm), pos[:, None])
    c = rms_norm(ckv, ckv_norm)
    k_pe = rope(rms_norm(kr, kr_norm), pos)
    k_nope = jnp.einsum('ntc,chd->nthd', c, w_uk).astype(F32)
    kscale = lax.rsqrt(jnp.mean(k_nope * k_nope, axis=-1) + EPS)
    return u, v, q_nope, q_pe, c, k_pe, k_nope, kscale


def blocked_causal_attention(q, k, v, scale):
    n, t, h, d = q.shape
    nqb = t // QBLK
    kt = jnp.transpose(k, (0, 2, 1, 3))
    vt = jnp.transpose(v, (0, 2, 1, 3))
    qb = jnp.transpose(q.reshape(n, nqb, QBLK, h, d), (1, 0, 3, 2, 4))
    kpos = jnp.arange(t, dtype=jnp.int32)

    def body(args):
        q_blk, s0 = args
        qpos = s0 + jnp.arange(QBLK, dtype=jnp.int32)
        s = jnp.einsum('nhqd,nhkd->nhqk', q_blk, kt) * scale
        p = masked_softmax(s, kpos[None, :] <= qpos[:, None])
        return jnp.einsum('nhqk,nhkd->nhqd', p, vt).astype(q.dtype)

    o = lax.map(body, (qb, jnp.arange(nqb, dtype=jnp.int32) * QBLK))
    return jnp.transpose(o, (1, 0, 3, 2, 4)).reshape(n, t, h, v.shape[-1])


def odd_prompt(xn, pos, w_in, w_out, ln_g, ln_b, sgu_w, sgu_b, cq_norm, ckv_norm, w_uq, w_uk, w_uv,
               qn_norm, qr_norm, kn_norm, kr_norm):
    n, t, _ = xn.shape
    u, v, q_nope, q_pe, c, k_pe, k_nope, kscale = odd_project(
        xn, pos, w_in, ln_g, ln_b, cq_norm, ckv_norm, w_uq, w_uk, qn_norm, qr_norm, kr_norm)
    sgu = sgu_mix(u, v, sgu_w, sgu_b)
    k_nope_n = (k_nope * kscale[..., None] * kn_norm.astype(F32)).astype(xn.dtype)
    q = jnp.concatenate([q_nope, q_pe], axis=-1)
    k = jnp.concatenate([k_nope_n, jnp.broadcast_to(k_pe[:, :, None, :], (n, t, MLA_HEADS, MLA_ROPE))], axis=-1)
    vv = jnp.einsum('ntc,chd->nthd', c, w_uv)
    att = blocked_causal_attention(q, k, vv, MLA_SCALE).reshape(n, t, MLA_HEADS * MLA_V)
    y = jnp.concatenate([sgu, att], axis=-1) @ w_out
    return y, (c, k_pe, kscale.astype(xn.dtype))


def odd_sample(xn, pos, cache_ckv, cache_kpe, cache_kscale, page_table, w_in, w_out, ln_g, ln_b, sgu_w,
               sgu_b, cq_norm, ckv_norm, w_uq, w_uk, w_uv, qn_norm, qr_norm, kn_norm, kr_norm):
    n, s, _ = xn.shape
    u, v, q_nope, q_pe, c, k_pe, k_nope, kscale = odd_project(
        xn, pos, w_in, ln_g, ln_b, cq_norm, ckv_norm, w_uq, w_uk, qn_norm, qr_norm, kr_norm)
    sgu = sgu_mix(u, v, sgu_w, sgu_b)
    q_lat = jnp.einsum('nqhd,chd->nqhc', q_nope.astype(F32) * kn_norm.astype(F32), w_uk)
    past = page_table.shape[1] * PAGE_SIZE
    c_past = cache_ckv[page_table].reshape(n, past, MLA_KV_LORA)
    kpe_past = cache_kpe[page_table].reshape(n, past, MLA_ROPE)
    ks_past = cache_kscale[page_table].reshape(n, past, MLA_HEADS)

    def scores(c_, kpe_, ks_):
        content = jnp.einsum('nqhc,nkc->nhqk', q_lat, c_) * jnp.swapaxes(ks_, 1, 2)[:, :, None, :].astype(F32)
        return (content + jnp.einsum('nqhr,nkr->nhqk', q_pe, kpe_)) * MLA_SCALE

    sc = jnp.concatenate([scores(c_past, kpe_past, ks_past), scores(c, k_pe, kscale)], axis=-1)
    mask = jnp.concatenate([jnp.ones((s, past), bool), jnp.tril(jnp.ones((s, s), bool))], axis=-1)
    p = masked_softmax(sc, mask)
    o_lat = jnp.einsum('nhqk,nkc->nqhc', p[..., :past], c_past) + jnp.einsum('nhqk,nkc->nqhc', p[..., past:], c)
    att = jnp.einsum('nqhc,chd->nqhd', o_lat, w_uv).astype(xn.dtype).reshape(n, s, MLA_HEADS * MLA_V)
    y = jnp.concatenate([sgu, att], axis=-1) @ w_out
    return y, (c, k_pe, kscale.astype(xn.dtype), v)


def setup_inputs(seed: int = 0) -> dict:
    keys = list(jax.random.split(jax.random.key(seed), 48))

    def nrm(shape, scale=1.0):
        return jax.random.normal(keys.pop(), shape, F32) * scale

    def gain(shape):
        return 1.0 + nrm(shape, 0.1)

    n_pages = PAST_LEN // PAGE_SIZE
    n_pool = (DEC_BATCH * n_pages * 5) // 4
    wbuf = min(NSA_WINDOW, PAST_LEN)
    page_table = jax.random.permutation(keys.pop(), n_pool)[: DEC_BATCH * n_pages]
    page_table = page_table.reshape(DEC_BATCH, n_pages).astype(jnp.int32)
    dt = jnp.exp(jax.random.uniform(keys.pop(), (GDN_HEADS,), F32, math.log(1e-3), math.log(1e-1)))
    return {
        'x_prompt': nrm((BATCH, SEQ, D_MODEL)),
        'x_sample': nrm((DEC_BATCH, DEC_SEQ, D_MODEL)),
        'state_gdn': nrm((DEC_BATCH, GDN_HEADS, GDN_DK, GDN_DV), 0.1),
        'state_gdn_conv': nrm((DEC_BATCH, GDN_CONV - 1, GDN_CONV_CH)),
        'cache_nsa_cmp': nrm((n_pool, 2, NSA_KV_HEADS, PAGE_SIZE, NSA_DH)),
        'cache_nsa_sel': nrm((n_pool, 2, NSA_KV_HEADS, PAGE_SIZE, NSA_DH)),
        'state_nsa_win': nrm((DEC_BATCH, 2, NSA_KV_HEADS, wbuf, NSA_DH)),
        'cache_mla_ckv': nrm((n_pool, PAGE_SIZE, MLA_KV_LORA)),
        'cache_mla_kpe': nrm((n_pool, PAGE_SIZE, MLA_ROPE)),
        'cache_mla_kscale': jax.random.uniform(keys.pop(), (n_pool, PAGE_SIZE, MLA_HEADS), F32, 0.5, 1.5),
        'page_table': page_table,
        'norm_mix': gain((DEPTH, D_MODEL)),
        'norm_ffn': gain((DEPTH, D_MODEL)),
        'ffn_up': nrm((DEPTH, D_MODEL, D_FF), D_MODEL ** -0.5),
        'ffn_down': nrm((DEPTH, D_FF, D_MODEL), D_FF ** -0.5),
        'w_in_even': nrm((D_MODEL, EVEN_COLS), D_MODEL ** -0.5),
        'w_out_even': nrm((EVEN_OUT, D_MODEL), EVEN_OUT ** -0.5),
        'gdn_conv_w': nrm((GDN_CONV, GDN_CONV_CH), GDN_CONV ** -0.5),
        'gdn_a_log': jnp.log(jax.random.uniform(keys.pop(), (GDN_HEADS,), F32, 1.0, 16.0)),
        'gdn_dt_bias': dt + jnp.log(-jnp.expm1(-dt)),
        'gdn_out_norm': gain((GDN_DV,)),
        'nsa_q_norm': gain((NSA_DH,)),
        'nsa_k_norm': gain((3, NSA_DH)),
        'w_in_odd': nrm((D_MODEL, ODD_COLS), D_MODEL ** -0.5),
        'w_out_odd': nrm((ODD_OUT, D_MODEL), ODD_OUT ** -0.5),
        'sgu_ln_g': gain((SGU_WIDTH,)),
        'sgu_ln_b': nrm((SGU_WIDTH,), 0.02),
        'sgu_w': nrm((SGU_GROUPS, SGU_CHUNK, SGU_CHUNK), SGU_CHUNK ** -0.5),
        'sgu_b': gain((SGU_GROUPS, SGU_CHUNK)),
        'mla_cq_norm': gain((MLA_Q_LORA,)),
        'mla_ckv_norm': gain((MLA_KV_LORA,)),
        'mla_w_uq': nrm((MLA_Q_LORA, MLA_HEADS * (MLA_NOPE + MLA_ROPE)), MLA_Q_LORA ** -0.5),
        'mla_w_uk': nrm((MLA_KV_LORA, MLA_HEADS, MLA_NOPE), MLA_KV_LORA ** -0.5),
        'mla_w_uv': nrm((MLA_KV_LORA, MLA_HEADS, MLA_V), MLA_KV_LORA ** -0.5),
        'mla_qn_norm': gain((MLA_NOPE,)),
        'mla_qr_norm': gain((MLA_ROPE,)),
        'mla_kn_norm': gain((MLA_NOPE,)),
        'mla_kr_norm': gain((MLA_ROPE,)),
    }


def reference(x_prompt, x_sample, state_gdn, state_gdn_conv, cache_nsa_cmp, cache_nsa_sel, state_nsa_win,
              cache_mla_ckv, cache_mla_kpe, cache_mla_kscale, page_table, norm_mix, norm_ffn, ffn_up, ffn_down,
              w_in_even, w_out_even, gdn_conv_w, gdn_a_log, gdn_dt_bias, gdn_out_norm, nsa_q_norm, nsa_k_norm,
              w_in_odd, w_out_odd, sgu_ln_g, sgu_ln_b, sgu_w, sgu_b, mla_cq_norm, mla_ckv_norm, mla_w_uq,
              mla_w_uk, mla_w_uv, mla_qn_norm, mla_qr_norm, mla_kn_norm, mla_kr_norm):
    pos_p = jnp.arange(x_prompt.shape[1], dtype=jnp.int32)
    pos_s = page_table.shape[1] * PAGE_SIZE + jnp.arange(x_sample.shape[1], dtype=jnp.int32)
    hp, hs = x_prompt, x_sample
    for layer in range(DEPTH):
        xp = rms_norm(hp, norm_mix[layer])
        xs = rms_norm(hs, norm_mix[layer])
        if layer % 2 == 0:
            mp, (p_gdn_state, p_gdn_conv, p_nsa_cmp, p_nsa_sel, p_nsa_win) = even_prompt(
                xp, pos_p, w_in_even, w_out_even, gdn_conv_w, gdn_a_log, gdn_dt_bias, gdn_out_norm,
                nsa_q_norm, nsa_k_norm)
            ms, (s_gdn_state, s_gdn_conv, s_nsa_cmp, s_nsa_sel, s_nsa_win) = even_sample(
                xs, pos_s, state_gdn, state_gdn_conv, cache_nsa_cmp, cache_nsa_sel, state_nsa_win, page_table,
                w_in_even, w_out_even, gdn_conv_w, gdn_a_log, gdn_dt_bias, gdn_out_norm, nsa_q_norm, nsa_k_norm)
        else:
            mp, (p_mla_ckv, p_mla_kpe, p_mla_kscale) = odd_prompt(
                xp, pos_p, w_in_odd, w_out_odd, sgu_ln_g, sgu_ln_b, sgu_w, sgu_b, mla_cq_norm, mla_ckv_norm,
                mla_w_uq, mla_w_uk, mla_w_uv, mla_qn_norm, mla_qr_norm, mla_kn_norm, mla_kr_norm)
            ms, (s_mla_ckv, s_mla_kpe, s_mla_kscale, s_sgu_v) = odd_sample(
                xs, pos_s, cache_mla_ckv, cache_mla_kpe, cache_mla_kscale, page_table, w_in_odd, w_out_odd,
                sgu_ln_g, sgu_ln_b, sgu_w, sgu_b, mla_cq_norm, mla_ckv_norm, mla_w_uq, mla_w_uk, mla_w_uv,
                mla_qn_norm, mla_qr_norm, mla_kn_norm, mla_kr_norm)
        hp = sq_relu_mlp(hp + mp, norm_ffn[layer], ffn_up[layer], ffn_down[layer])
        hs = sq_relu_mlp(hs + ms, norm_ffn[layer], ffn_up[layer], ffn_down[layer])
    return (hp, hs, p_gdn_state, p_gdn_conv, p_nsa_cmp, p_nsa_sel, p_nsa_win, p_mla_ckv, p_mla_kpe, p_mla_kscale,
            s_gdn_state, s_gdn_conv, s_nsa_cmp, s_nsa_sel, s_nsa_win, s_mla_ckv, s_mla_kpe, s_mla_kscale, s_sgu_v)
```

```python
import functools
import math

import jax
import jax.numpy as jnp
import numpy as np
from jax import lax
from jax.experimental import pallas as pl
from jax.experimental.pallas import tpu as pltpu

F32 = jnp.float32
BF16 = jnp.bfloat16

D_MODEL = 1024
PAGE_SIZE = 128
ROPE_THETA = 500000.0
EPS = 1e-6
QBLK = 128

GDN_HEADS = 4
GDN_DK = 128
GDN_DV = 128
GDN_CONV = 4
GDN_CHUNK = 64
GDN_CONV_CH = GDN_HEADS * (2 * GDN_DK + GDN_DV)

NSA_HEADS = 8
NSA_KV_HEADS = 2
NSA_GROUP = NSA_HEADS // NSA_KV_HEADS
NSA_DH = 64
NSA_ROT = NSA_DH // 4
NSA_BLK = 64
NSA_TOPN = 16
NSA_LOCAL = 2
NSA_WINDOW = 512
NSA_FORCE = 1.0e4
NSA_SCALE = NSA_DH ** -0.5

SGU_GROUPS = 4
SGU_DG = 128
SGU_CHUNK = 128
SGU_WIDTH = SGU_GROUPS * SGU_DG

MLA_HEADS = 8
MLA_Q_LORA = 256
MLA_KV_LORA = 256
MLA_NOPE = 64
MLA_ROPE = 32
MLA_V = 64
MLA_SCALE = (MLA_NOPE + MLA_ROPE) ** -0.5

D_FF = 4 * D_MODEL

EVEN_SIZES = (GDN_CONV_CH, GDN_HEADS * GDN_DV, GDN_HEADS, GDN_HEADS,
              NSA_HEADS * NSA_DH, 3 * 2 * NSA_KV_HEADS * NSA_DH, 3 * NSA_HEADS)
ODD_SIZES = (2 * SGU_WIDTH, MLA_Q_LORA, MLA_KV_LORA, MLA_ROPE)

V7X_VMEM_LIMIT_BYTES = 56 * 1024 * 1024


def _mlp_body(x_ref, g_ref, wu_ref, wd_ref, o_ref, xn_ref, acc_ref):
    f = pl.program_id(1)

    @pl.when(f == 0)
    def _():
        x = x_ref[...]
        y = x * lax.rsqrt(jnp.mean(x * x, axis=-1, keepdims=True) + EPS)
        xn_ref[...] = (y * g_ref[...]).astype(BF16)
        acc_ref[...] = jnp.zeros_like(acc_ref)

    z = jnp.dot(xn_ref[...], wu_ref[...], preferred_element_type=F32)
    a = jnp.square(jnp.maximum(z, 0.0)).astype(BF16)
    acc_ref[...] += jnp.dot(a, wd_ref[...], preferred_element_type=F32)

    @pl.when(f == pl.num_programs(1) - 1)
    def _():
        o_ref[...] = x_ref[...] + acc_ref[...]


def mlp_tiles(t):
    tm = min(t, 1024)
    tf = 512
    return tm, tf


def sq_relu_mlp(h, g, w_up, w_down):
    n, t, d = h.shape
    rows = n * t
    x = h.reshape(rows, d)
    tm, tf = mlp_tiles(rows)
    dff = w_up.shape[1]
    out = pl.pallas_call(
        _mlp_body,
        grid=(rows // tm, dff // tf),
        in_specs=[
            pl.BlockSpec((tm, d), lambda i, f: (i, 0)),
            pl.BlockSpec((1, d), lambda i, f: (0, 0)),
            pl.BlockSpec((d, tf), lambda i, f: (0, f)),
            pl.BlockSpec((tf, d), lambda i, f: (f, 0)),
        ],
        out_specs=pl.BlockSpec((tm, d), lambda i, f: (i, 0)),
        out_shape=jax.ShapeDtypeStruct((rows, d), F32),
        scratch_shapes=[pltpu.VMEM((tm, d), BF16), pltpu.VMEM((tm, d), F32)],
        compiler_params=pltpu.CompilerParams(
            dimension_semantics=("parallel", "arbitrary"),
            vmem_limit_bytes=V7X_VMEM_LIMIT_BYTES),
        name="sq_relu_mlp",
    )(x, g.reshape(1, d), w_up, w_down)
    return out.reshape(n, t, d)


def split_cols(x, sizes):
    return jnp.split(x, np.cumsum(sizes)[:-1].tolist(), axis=-1)


def rms_norm(x, g):
    xf = x.astype(F32)
    y = xf * lax.rsqrt(jnp.mean(xf * xf, axis=-1, keepdims=True) + EPS)
    return (y * g.astype(F32)).astype(x.dtype)


def layer_norm(x, g, b):
    xf = x.astype(F32)
    xc = xf - jnp.mean(xf, axis=-1, keepdims=True)
    y = xc * lax.rsqrt(jnp.mean(xc * xc, axis=-1, keepdims=True) + EPS)
    return (y * g.astype(F32) + b.astype(F32)).astype(x.dtype)


def l2_normalize(x):
    xf = x.astype(F32)
    return xf * lax.rsqrt(jnp.sum(xf * xf, axis=-1, keepdims=True) + EPS)


def rope(x, pos):
    d = x.shape[-1]
    inv = ROPE_THETA ** (-jnp.arange(0, d, 2, dtype=F32) / d)
    ang = pos.astype(F32)[..., None] * inv
    cos, sin = jnp.cos(ang), jnp.sin(ang)
    xf = x.astype(F32)
    x1, x2 = xf[..., : d // 2], xf[..., d // 2:]
    return jnp.concatenate([x1 * cos - x2 * sin, x2 * cos + x1 * sin], axis=-1).astype(x.dtype)


def partial_rope(x, pos, rot):
    return jnp.concatenate([rope(x[..., :rot], pos), x[..., rot:]], axis=-1)


def masked_softmax(s, mask):
    s = jnp.where(mask, s.astype(F32), -1e30)
    return jax.nn.softmax(s, axis=-1) * jnp.any(mask, axis=-1, keepdims=True)


def causal_conv(x_ext, w):
    t = x_ext.shape[1] - (GDN_CONV - 1)
    return sum(x_ext[:, i:i + t] * w[i] for i in range(GDN_CONV))


def gdn_prepare(conv_out, b_raw, a_raw, a_log, dt_bias):
    n, t, _ = conv_out.shape
    q, k, v = split_cols(conv_out, (GDN_HEADS * GDN_DK, GDN_HEADS * GDN_DK, GDN_HEADS * GDN_DV))
    q = l2_normalize(q.reshape(n, t, GDN_HEADS, GDN_DK)) * (GDN_DK ** -0.5)
    k = l2_normalize(k.reshape(n, t, GDN_HEADS, GDN_DK))
    v = v.reshape(n, t, GDN_HEADS, GDN_DV).astype(F32)
    beta = jax.nn.sigmoid(b_raw.astype(F32))
    g = -jnp.exp(a_log.astype(F32)) * jax.nn.softplus(a_raw.astype(F32) + dt_bias.astype(F32))
    return q, k, v, beta, g


def gdn_chunked(q, k, v, beta, g):
    n, t, h, _ = q.shape
    c = GDN_CHUNK
    nc = t // c

    def chunks(x):
        return jnp.moveaxis(x.reshape((n, nc, c, h) + x.shape[3:]), 3, 1)

    q, k, v, beta, g = map(chunks, (q, k, v, beta, g))
    gc = jnp.cumsum(g, axis=-1)
    tril = jnp.tril(jnp.ones((c, c), bool))
    stril = jnp.tril(jnp.ones((c, c), bool), -1)
    diff = gc[..., :, None] - gc[..., None, :]
    decay = jnp.where(tril, jnp.exp(jnp.where(tril, diff, 0.0)), 0.0)
    kb = k * beta[..., None]
    m = jnp.where(stril, jnp.einsum('nhcid,nhcjd->nhcij', kb, k) * decay, 0.0)
    eye = jnp.eye(c, dtype=F32)
    tinv = lax.linalg.triangular_solve(eye + m, jnp.broadcast_to(eye, m.shape),
                                       left_side=True, lower=True, unit_diagonal=True)
    u = tinv @ (v * beta[..., None])
    w = tinv @ (kb * jnp.exp(gc)[..., None])
    attn = jnp.where(tril, jnp.einsum('nhcid,nhcjd->nhcij', q, k) * decay, 0.0)
    qg = q * jnp.exp(gc)[..., None]
    kd = k * jnp.exp(gc[..., -1:] - gc)[..., None]
    gl = jnp.exp(gc[..., -1])

    def step(s, xs):
        u_c, w_c, a_c, qg_c, kd_c, gl_c = xs
        v_new = u_c - jnp.einsum('nhcd,nhde->nhce', w_c, s)
        o_c = jnp.einsum('nhcd,nhde->nhce', qg_c, s) + jnp.einsum('nhij,nhje->nhie', a_c, v_new)
        s = s * gl_c[..., None, None] + jnp.einsum('nhcd,nhce->nhde', kd_c, v_new)
        return s, o_c

    xs = tuple(jnp.moveaxis(x, 2, 0) for x in (u, w, attn, qg, kd, gl))
    s0 = jnp.zeros((n, h, GDN_DK, GDN_DV), F32)
    s_fin, o = lax.scan(step, s0, xs)
    o = jnp.transpose(o, (1, 0, 3, 2, 4)).reshape(n, t, h, GDN_DV)
    return o, s_fin


def gdn_recurrent(q, k, v, beta, g, s0):
    def step(s, xs):
        q_t, k_t, v_t, b_t, g_t = xs
        s = s * jnp.exp(g_t)[..., None, None]
        delta = (v_t - jnp.einsum('nhde,nhd->nhe', s, k_t)) * b_t[..., None]
        s = s + jnp.einsum('nhd,nhe->nhde', k_t, delta)
        return s, jnp.einsum('nhde,nhd->nhe', s, q_t)

    xs = tuple(jnp.moveaxis(x, 1, 0) for x in (q, k, v, beta, g))
    s_fin, o = lax.scan(step, s0, xs)
    return jnp.moveaxis(o, 0, 1), s_fin


def gdn_output(o, z, out_norm):
    n, t = z.shape[:2]
    gate = jax.nn.silu(z.reshape(n, t, GDN_HEADS, GDN_DV).astype(F32))
    return (rms_norm(o, out_norm) * gate).astype(z.dtype).reshape(n, t, GDN_HEADS * GDN_DV)


def nsa_core(q, qpos, ckv, fetch, n_cand, wkv, wpos, gates):
    n, hk, gr, nq, _ = q.shape
    nbc = ckv.shape[3]
    s_c = jnp.einsum('nhgqd,nhjd->nhgqj', q, ckv[:, 0]) * NSA_SCALE
    cmask = (jnp.arange(nbc)[None, :] + 1) * NSA_BLK <= qpos[:, None] + 1
    p_c = masked_softmax(s_c, cmask)
    o_c = jnp.einsum('nhgqj,nhjd->nhgqd', p_c, ckv[:, 1])
    imp = jnp.pad(p_c.sum(axis=2), ((0, 0), (0, 0), (0, 0), (0, n_cand - nbc)))
    j = jnp.arange(n_cand)[None, :]
    cur = (qpos // NSA_BLK)[:, None]
    forced = (j == 0) | ((j > cur - NSA_LOCAL) & (j <= cur))
    score = jnp.where(j <= cur, jnp.where(forced, NSA_FORCE, imp), -jnp.inf)
    _, idx = lax.top_k(score, min(NSA_TOPN, n_cand))
    n_top = idx.shape[-1]
    sel = fetch(idx)
    kpos = idx[..., None] * NSA_BLK + jnp.arange(NSA_BLK)
    smask = (kpos <= qpos[:, None, None]).reshape(n, hk, 1, nq, n_top * NSA_BLK)
    s_s = jnp.einsum('nhgqd,nhqjkd->nhgqjk', q, sel[..., 0, :, :]) * NSA_SCALE
    p_s = masked_softmax(s_s.reshape(n, hk, gr, nq, n_top * NSA_BLK), smask)
    o_s = jnp.einsum('nhgqjk,nhqjkd->nhgqd', p_s.reshape(n, hk, gr, nq, n_top, NSA_BLK), sel[..., 1, :, :])
    s_w = jnp.einsum('nhgqd,nhkd->nhgqk', q, wkv[:, 0]) * NSA_SCALE
    dpos = qpos[:, None] - wpos[None, :]
    wmask = (dpos >= 0) & (dpos <= NSA_WINDOW) & (wpos >= 0)[None, :]
    o_w = jnp.einsum('nhgqk,nhkd->nhgqd', masked_softmax(s_w, wmask), wkv[:, 1])
    g = gates.astype(F32)
    return (g[..., 0:1] * o_c + g[..., 1:2] * o_s + g[..., 2:3] * o_w).astype(q.dtype)


def nsa_prompt(q, kv, gates):
    n, _, _, t, _ = q.shape
    nb = t // NSA_BLK
    ckv = kv[:, 0].reshape(n, 2, NSA_KV_HEADS, nb, NSA_BLK, NSA_DH).mean(axis=4, dtype=F32).astype(q.dtype)
    kvb = kv[:, 1].reshape(n, 2, NSA_KV_HEADS, nb, NSA_BLK, NSA_DH)

    def fetch(idx):
        bi = jnp.arange(n)[:, None, None, None, None]
        hi = jnp.arange(NSA_KV_HEADS)[None, :, None, None, None]
        return kvb[bi, jnp.arange(2), hi, idx[..., None]]

    wpad = jnp.pad(kv[:, 2], ((0, 0), (0, 0), (0, 0), (NSA_WINDOW, 0), (0, 0)))
    nqb = t // QBLK
    qb = jnp.moveaxis(q.reshape(n, NSA_KV_HEADS, NSA_GROUP, nqb, QBLK, NSA_DH), 3, 0)
    gb = jnp.moveaxis(gates.reshape(n, NSA_KV_HEADS, NSA_GROUP, nqb, QBLK, 3), 3, 0)

    def body(args):
        q_blk, g_blk, s0 = args
        qpos = s0 + jnp.arange(QBLK, dtype=jnp.int32)
        wkv = lax.dynamic_slice_in_dim(wpad, s0, NSA_WINDOW + QBLK, axis=3)
        wpos = s0 - NSA_WINDOW + jnp.arange(NSA_WINDOW + QBLK, dtype=jnp.int32)
        return nsa_core(q_blk, qpos, ckv, fetch, nb, wkv, wpos, g_blk)

    o = lax.map(body, (qb, gb, jnp.arange(nqb, dtype=jnp.int32) * QBLK))
    o = jnp.transpose(o, (1, 0, 4, 2, 3, 5)).reshape(n, t, NSA_HEADS * NSA_DH)
    return o, kv[:, 2, :, :, t - min(NSA_WINDOW, t):]


def nsa_sample(q, kv, gates, cache_cmp, cache_sel, win_state, page_table, pos):
    n, _, _, s, _ = q.shape
    n_pool, npg = cache_cmp.shape[0], page_table.shape[1]
    bpp = PAGE_SIZE // NSA_BLK
    nbp = npg * bpp
    past_len = npg * PAGE_SIZE
    shp = (n_pool, 2, NSA_KV_HEADS, bpp, NSA_BLK, NSA_DH)
    means = cache_cmp.reshape(shp).mean(axis=4, dtype=F32).astype(q.dtype)[page_table]
    ck_past = jnp.transpose(means, (0, 2, 3, 1, 4, 5)).reshape(n, 2, NSA_KV_HEADS, nbp, NSA_DH)
    nfull = s // NSA_BLK
    ck_new = kv[:, 0, :, :, :nfull * NSA_BLK].reshape(n, 2, NSA_KV_HEADS, nfull, NSA_BLK, NSA_DH)
    ckv = jnp.concatenate([ck_past, ck_new.mean(axis=4, dtype=F32).astype(q.dtype)], axis=3)
    nnew = -(-s // NSA_BLK)
    newb = jnp.pad(kv[:, 1], ((0, 0), (0, 0), (0, 0), (0, nnew * NSA_BLK - s), (0, 0)))
    newb = newb.reshape(n, 2, NSA_KV_HEADS, nnew, NSA_BLK, NSA_DH)
    pool = cache_sel.reshape(shp)

    def fetch(idx):
        bi = jnp.arange(n)[:, None, None, None]
        hi = jnp.arange(NSA_KV_HEADS)[None, :, None, None]
        kvi = jnp.arange(2)
        phys = page_table[bi, jnp.clip(idx // bpp, 0, npg - 1)]
        past = pool[phys[..., None], kvi, hi[..., None], (idx % bpp)[..., None]]
        new = newb[bi[..., None], kvi, hi[..., None], jnp.clip(idx - nbp, 0, nnew - 1)[..., None]]
        return jnp.where((idx < nbp)[..., None, None, None], past, new)

    wb = win_state.shape[3]
    wkv = jnp.concatenate([win_state.astype(kv.dtype), kv[:, 2]], axis=3)
    wpos = past_len - wb + jnp.arange(wb + s, dtype=jnp.int32)
    o = nsa_core(q, pos, ckv, fetch, nbp + nnew, wkv, wpos, gates)
    o = jnp.transpose(o, (0, 3, 1, 2, 4)).reshape(n, s, NSA_HEADS * NSA_DH)
    return o, wkv[:, :, :, wb + s - min(NSA_WINDOW, past_len + s):]


def even_project(xn, pos, w_in, q_norm, k_norm):
    n, t, _ = xn.shape
    qkv, z, b_raw, a_raw, q, kv, g_raw = split_cols(xn @ w_in, EVEN_SIZES)
    q = partial_rope(rms_norm(q.reshape(n, t, NSA_HEADS, NSA_DH), q_norm), pos[:, None], NSA_ROT)
    q = jnp.transpose(q.reshape(n, t, NSA_KV_HEADS, NSA_GROUP, NSA_DH), (0, 2, 3, 1, 4))
    kv = kv.reshape(n, t, 3, 2, NSA_KV_HEADS, NSA_DH)
    k = partial_rope(rms_norm(kv[:, :, :, 0], k_norm[:, None, :]), pos[:, None, None], NSA_ROT)
    kv = jnp.transpose(jnp.stack([k, kv[:, :, :, 1]], axis=3), (0, 2, 3, 4, 1, 5))
    gates = jax.nn.sigmoid(g_raw.astype(F32)).reshape(n, t, NSA_KV_HEADS, NSA_GROUP, 3)
    return qkv, z, b_raw, a_raw, q, kv, jnp.transpose(gates, (0, 2, 3, 1, 4))


def even_prompt(xn, pos, w_in, w_out, conv_w, a_log, dt_bias, out_norm, q_norm, k_norm):
    n = xn.shape[0]
    qkv, z, b_raw, a_raw, q, kv, gates = even_project(xn, pos, w_in, q_norm, k_norm)
    qkv_ext = jnp.concatenate([jnp.zeros((n, GDN_CONV - 1, GDN_CONV_CH), qkv.dtype), qkv], axis=1)
    gq, gk, gv, beta, g = gdn_prepare(jax.nn.silu(causal_conv(qkv_ext, conv_w)), b_raw, a_raw, a_log, dt_bias)
    o, s_fin = gdn_chunked(gq, gk, gv, beta, g)
    nsa_out, win = nsa_prompt(q, kv, gates)
    y = jnp.concatenate([gdn_output(o, z, out_norm), nsa_out], axis=-1) @ w_out
    return y, (s_fin.astype(xn.dtype), qkv_ext[:, -(GDN_CONV - 1):], kv[:, 0], kv[:, 1], win)


def even_sample(xn, pos, gdn_state, conv_state, cache_cmp, cache_sel, win_state, page_table,
                w_in, w_out, conv_w, a_log, dt_bias, out_norm, q_norm, k_norm):
    qkv, z, b_raw, a_raw, q, kv, gates = even_project(xn, pos, w_in, q_norm, k_norm)
    qkv_ext = jnp.concatenate([conv_state.astype(qkv.dtype), qkv], axis=1)
    gq, gk, gv, beta, g = gdn_prepare(jax.nn.silu(causal_conv(qkv_ext, conv_w)), b_raw, a_raw, a_log, dt_bias)
    o, s_fin = gdn_recurrent(gq, gk, gv, beta, g, gdn_state.astype(F32))
    nsa_out, win = nsa_sample(q, kv, gates, cache_cmp, cache_sel, win_state, page_table, pos)
    y = jnp.concatenate([gdn_output(o, z, out_norm), nsa_out], axis=-1) @ w_out
    return y, (s_fin.astype(xn.dtype), qkv_ext[:, -(GDN_CONV - 1):], kv[:, 0], kv[:, 1], win)


def sgu_mix(u, v, w_s, b_s):
    n, t, _ = u.shape
    l = min(t, SGU_CHUNK)
    nc = t // l
    w = jnp.tril(w_s[:, :l, :l])
    mix = jnp.einsum('gts,ncsgd->nctgd', w, v.reshape(n, nc, l, SGU_GROUPS, SGU_DG))
    mix = mix + jnp.transpose(b_s[:, :l])[:, :, None]
    return (u.reshape(n, nc, l, SGU_GROUPS, SGU_DG) * mix).reshape(n, t, SGU_WIDTH)


def odd_project(xn, pos, w_in, ln_g, ln_b, cq_norm, ckv_norm, w_uq, w_uk, qn_norm, qr_norm, kr_norm):
    n, t, _ = xn.shape
    uv, cq, ckv, kr = split_cols(xn @ w_in, ODD_SIZES)
    uv = jax.nn.gelu(uv, approximate=False)
    u = uv[..., :SGU_WIDTH]
    v = layer_norm(uv[..., SGU_WIDTH:], ln_g, ln_b)
    q = (rms_norm(cq, cq_norm) @ w_uq).reshape(n, t, MLA_HEADS, MLA_NOPE + MLA_ROPE)
    q_nope = rms_norm(q[..., :MLA_NOPE], qn_norm)
    q_pe = rope(rms_norm(q[..., MLA_NOPE:], qr_norm), pos[:, None])
    c = rms_norm(ckv, ckv_norm)
    k_pe = rope(rms_norm(kr, kr_norm), pos)
    k_nope = jnp.einsum('ntc,chd->nthd', c, w_uk).astype(F32)
    kscale = lax.rsqrt(jnp.mean(k_nope * k_nope, axis=-1) + EPS)
    return u, v, q_nope, q_pe, c, k_pe, k_nope, kscale


def blocked_causal_attention(q, k, v, scale):
    n, t, h, d = q.shape
    nqb = t // QBLK
    kt = jnp.transpose(k, (0, 2, 1, 3))
    vt = jnp.transpose(v, (0, 2, 1, 3))
    qb = jnp.transpose(q.reshape(n, nqb, QBLK, h, d), (1, 0, 3, 2, 4))
    kpos = jnp.arange(t, dtype=jnp.int32)

    def body(args):
        q_blk, s0 = args
        qpos = s0 + jnp.arange(QBLK, dtype=jnp.int32)
        s = jnp.einsum('nhqd,nhkd->nhqk', q_blk, kt) * scale
        p = masked_softmax(s, kpos[None, :] <= qpos[:, None])
        return jnp.einsum('nhqk,nhkd->nhqd', p, vt).astype(q.dtype)

    o = lax.map(body, (qb, jnp.arange(nqb, dtype=jnp.int32) * QBLK))
    return jnp.transpose(o, (1, 0, 3, 2, 4)).reshape(n, t, h, v.shape[-1])


def odd_prompt(xn, pos, w_in, w_out, ln_g, ln_b, sgu_w, sgu_b, cq_norm, ckv_norm, w_uq, w_uk, w_uv,
               qn_norm, qr_norm, kn_norm, kr_norm):
    n, t, _ = xn.shape
    u, v, q_nope, q_pe, c, k_pe, k_nope, kscale = odd_project(
        xn, pos, w_in, ln_g, ln_b, cq_norm, ckv_norm, w_uq, w_uk, qn_norm, qr_norm, kr_norm)
    sgu = sgu_mix(u, v, sgu_w, sgu_b)
    k_nope_n = (k_nope * kscale[..., None] * kn_norm.astype(F32)).astype(xn.dtype)
    q = jnp.concatenate([q_nope, q_pe], axis=-1)
    k = jnp.concatenate([k_nope_n, jnp.broadcast_to(k_pe[:, :, None, :], (n, t, MLA_HEADS, MLA_ROPE))], axis=-1)
    vv = jnp.einsum('ntc,chd->nthd', c, w_uv)
    att = blocked_causal_attention(q, k, vv, MLA_SCALE).reshape(n, t, MLA_HEADS * MLA_V)
    y = jnp.concatenate([sgu, att], axis=-1) @ w_out
    return y, (c, k_pe, kscale.astype(xn.dtype))


def odd_sample(xn, pos, cache_ckv, cache_kpe, cache_kscale, page_table, w_in, w_out, ln_g, ln_b, sgu_w,
               sgu_b, cq_norm, ckv_norm, w_uq, w_uk, w_uv, qn_norm, qr_norm, kn_norm, kr_norm):
    n, s, _ = xn.shape
    u, v, q_nope, q_pe, c, k_pe, k_nope, kscale = odd_project(
        xn, pos, w_in, ln_g, ln_b, cq_norm, ckv_norm, w_uq, w_uk, qn_norm, qr_norm, kr_norm)
    sgu = sgu_mix(u, v, sgu_w, sgu_b)
    q_lat = jnp.einsum('nqhd,chd->nqhc', q_nope.astype(F32) * kn_norm.astype(F32), w_uk)
    past = page_table.shape[1] * PAGE_SIZE
    c_past = cache_ckv[page_table].reshape(n, past, MLA_KV_LORA)
    kpe_past = cache_kpe[page_table].reshape(n, past, MLA_ROPE)
    ks_past = cache_kscale[page_table].reshape(n, past, MLA_HEADS)

    def scores(c_, kpe_, ks_):
        content = jnp.einsum('nqhc,nkc->nhqk', q_lat, c_) * jnp.swapaxes(ks_, 1, 2)[:, :, None, :].astype(F32)
        return (content + jnp.einsum('nqhr,nkr->nhqk', q_pe, kpe_)) * MLA_SCALE

    sc = jnp.concatenate([scores(c_past, kpe_past, ks_past), scores(c, k_pe, kscale)], axis=-1)
    mask = jnp.concatenate([jnp.ones((s, past), bool), jnp.tril(jnp.ones((s, s), bool))], axis=-1)
    p = masked_softmax(sc, mask)
    o_lat = jnp.einsum('nhqk,nkc->nqhc', p[..., :past], c_past) + jnp.einsum('nhqk,nkc->nqhc', p[..., past:], c)
    att = jnp.einsum('nqhc,chd->nqhd', o_lat, w_uv).astype(xn.dtype).reshape(n, s, MLA_HEADS * MLA_V)
    y = jnp.concatenate([sgu, att], axis=-1) @ w_out
    return y, (c, k_pe, kscale.astype(xn.dtype), v)


def kernel(x_prompt, x_sample, state_gdn, state_gdn_conv, cache_nsa_cmp, cache_nsa_sel, state_nsa_win,
           cache_mla_ckv, cache_mla_kpe, cache_mla_kscale, page_table, norm_mix, norm_ffn, ffn_up, ffn_down,
           w_in_even, w_out_even, gdn_conv_w, gdn_a_log, gdn_dt_bias, gdn_out_norm, nsa_q_norm, nsa_k_norm,
           w_in_odd, w_out_odd, sgu_ln_g, sgu_ln_b, sgu_w, sgu_b, mla_cq_norm, mla_ckv_norm, mla_w_uq,
           mla_w_uk, mla_w_uv, mla_qn_norm, mla_qr_norm, mla_kn_norm, mla_kr_norm):
    pos_p = jnp.arange(x_prompt.shape[1], dtype=jnp.int32)
    pos_s = page_table.shape[1] * PAGE_SIZE + jnp.arange(x_sample.shape[1], dtype=jnp.int32)
    ffn_up_b = ffn_up.astype(BF16)
    ffn_down_b = ffn_down.astype(BF16)
    hp, hs = x_prompt, x_sample
    for layer in range(norm_mix.shape[0]):
        xp = rms_norm(hp, norm_mix[layer])
        xs = rms_norm(hs, norm_mix[layer])
        if layer % 2 == 0:
            mp, (p_gdn_state, p_gdn_conv, p_nsa_cmp, p_nsa_sel, p_nsa_win) = even_prompt(
                xp, pos_p, w_in_even, w_out_even, gdn_conv_w, gdn_a_log, gdn_dt_bias, gdn_out_norm,
                nsa_q_norm, nsa_k_norm)
            ms, (s_gdn_state, s_gdn_conv, s_nsa_cmp, s_nsa_sel, s_nsa_win) = even_sample(
                xs, pos_s, state_gdn, state_gdn_conv, cache_nsa_cmp, cache_nsa_sel, state_nsa_win, page_table,
                w_in_even, w_out_even, gdn_conv_w, gdn_a_log, gdn_dt_bias, gdn_out_norm, nsa_q_norm, nsa_k_norm)
        else:
            mp, (p_mla_ckv, p_mla_kpe, p_mla_kscale) = odd_prompt(
                xp, pos_p, w_in_odd, w_out_odd, sgu_ln_g, sgu_ln_b, sgu_w, sgu_b, mla_cq_norm, mla_ckv_norm,
                mla_w_uq, mla_w_uk, mla_w_uv, mla_qn_norm, mla_qr_norm, mla_kn_norm, mla_kr_norm)
            ms, (s_mla_ckv, s_mla_kpe, s_mla_kscale, s_sgu_v) = odd_sample(
                xs, pos_s, cache_mla_ckv, cache_mla_kpe, cache_mla_kscale, page_table, w_in_odd, w_out_odd,
                sgu_ln_g, sgu_ln_b, sgu_w, sgu_b, mla_cq_norm, mla_ckv_norm, mla_w_uq, mla_w_uk, mla_w_uv,
                mla_qn_norm, mla_qr_norm, mla_kn_norm, mla_kr_norm)
        hp = sq_relu_mlp(hp + mp, norm_ffn[layer], ffn_up_b[layer], ffn_down_b[layer])
        hs = sq_relu_mlp(hs + ms, norm_ffn[layer], ffn_up_b[layer], ffn_down_b[layer])
    return (hp, hs, p_gdn_state, p_gdn_conv, p_nsa_cmp, p_nsa_sel, p_nsa_win, p_mla_ckv, p_mla_kpe, p_mla_kscale,
            s_gdn_state, s_gdn_conv, s_nsa_cmp, s_nsa_sel, s_nsa_win, s_mla_ckv, s_mla_kpe, s_mla_kscale, s_sgu_v)
```

```python
import functools

import jax
import jax.numpy as jnp
import numpy as np
from jax import lax
from jax.experimental import pallas as pl
from jax.experimental.pallas import tpu as pltpu

F32 = jnp.float32
BF16 = jnp.bfloat16

D_MODEL = 1024
PAGE_SIZE = 128
ROPE_THETA = 500000.0
EPS = 1e-6

GDN_HEADS = 4
GDN_DK = 128
GDN_DV = 128
GDN_CONV = 4
GDN_CHUNK = 64
GDN_CONV_CH = GDN_HEADS * (2 * GDN_DK + GDN_DV)

NSA_HEADS = 8
NSA_KV_HEADS = 2
NSA_GROUP = NSA_HEADS // NSA_KV_HEADS
NSA_DH = 64
NSA_ROT = NSA_DH // 4
NSA_BLK = 64
NSA_TOPN = 16
NSA_LOCAL = 2
NSA_WINDOW = 512
NSA_SCALE = NSA_DH ** -0.5

SGU_GROUPS = 4
SGU_DG = 128
SGU_CHUNK = 128
SGU_WIDTH = SGU_GROUPS * SGU_DG

MLA_HEADS = 8
MLA_Q_LORA = 256
MLA_KV_LORA = 256
MLA_NOPE = 64
MLA_ROPE = 32
MLA_V = 64
MLA_SCALE = (MLA_NOPE + MLA_ROPE) ** -0.5

EVEN_SIZES = (GDN_CONV_CH, GDN_HEADS * GDN_DV, GDN_HEADS, GDN_HEADS,
              NSA_HEADS * NSA_DH, 3 * 2 * NSA_KV_HEADS * NSA_DH, 3 * NSA_HEADS)
ODD_SIZES = (2 * SGU_WIDTH, MLA_Q_LORA, MLA_KV_LORA, MLA_ROPE)

V7X_VMEM_LIMIT_BYTES = 56 * 1024 * 1024
LANES = 128
NEG_BIG = -1e30


def _dot_nt(a, b):
    return lax.dot_general(a, b, (((1,), (1,)), ((), ())), preferred_element_type=F32)


def _bf16_round(x):
    return x.astype(BF16).astype(F32)


def _col_from_row(row):
    k = row.shape[1]
    r = lax.broadcasted_iota(jnp.int32, (k, k), 0)
    c = lax.broadcasted_iota(jnp.int32, (k, k), 1)
    return jnp.sum(jnp.where(r == c, jnp.broadcast_to(row, (k, k)), 0.0), axis=1, keepdims=True)


def _mlp_body(x_ref, g_ref, wu_ref, wd_ref, o_ref, xn_ref, acc_ref):
    f = pl.program_id(1)

    @pl.when(f == 0)
    def _():
        x = x_ref[...]
        y = x * lax.rsqrt(jnp.mean(x * x, axis=-1, keepdims=True) + EPS)
        xn_ref[...] = (y * g_ref[...]).astype(BF16)
        acc_ref[...] = jnp.zeros_like(acc_ref)

    z = jnp.dot(xn_ref[...], wu_ref[...], preferred_element_type=F32)
    a = jnp.square(jnp.maximum(z, 0.0)).astype(BF16)
    acc_ref[...] += jnp.dot(a, wd_ref[...], preferred_element_type=F32)

    @pl.when(f == pl.num_programs(1) - 1)
    def _():
        o_ref[...] = x_ref[...] + acc_ref[...]


def mlp_tiles(t):
    tm = min(t, 1024)
    tf = 512
    return tm, tf


def sq_relu_mlp(h, g, w_up, w_down):
    n, t, d = h.shape
    rows = n * t
    x = h.reshape(rows, d)
    tm, tf = mlp_tiles(rows)
    dff = w_up.shape[1]
    out = pl.pallas_call(
        _mlp_body,
        grid=(rows // tm, dff // tf),
        in_specs=[
            pl.BlockSpec((tm, d), lambda i, f: (i, 0)),
            pl.BlockSpec((1, d), lambda i, f: (0, 0)),
            pl.BlockSpec((d, tf), lambda i, f: (0, f)),
            pl.BlockSpec((tf, d), lambda i, f: (f, 0)),
        ],
        out_specs=pl.BlockSpec((tm, d), lambda i, f: (i, 0)),
        out_shape=jax.ShapeDtypeStruct((rows, d), F32),
        scratch_shapes=[pltpu.VMEM((tm, d), BF16), pltpu.VMEM((tm, d), F32)],
        compiler_params=pltpu.CompilerParams(
            dimension_semantics=("parallel", "arbitrary"),
            vmem_limit_bytes=V7X_VMEM_LIMIT_BYTES),
        name="sq_relu_mlp",
    )(x, g.reshape(1, d), w_up, w_down)
    return out.reshape(n, t, d)


FLASH_TQ = 512
FLASH_TK = 512


def _online_softmax_step(s, v, m, l, acc):
    tk = s.shape[1]
    m_new = jnp.maximum(m, jnp.max(s, axis=1, keepdims=True))
    alpha = jnp.exp(m - m_new)
    p = jnp.exp(s - pltpu.repeat(m_new, tk // LANES, axis=1))
    l_new = alpha * l + jnp.sum(p, axis=1, keepdims=True)
    pv = jnp.dot(p.astype(BF16), v, preferred_element_type=F32)
    acc_new = pltpu.repeat(alpha, acc.shape[1] // LANES, axis=1) * acc + pv
    return m_new, l_new, acc_new


def _flash_pair_body(q_ref, k_ref, v_ref, o_ref, *, tq, tk, dv):
    i = pl.program_id(1)
    q = [q_ref[:, h * LANES:(h + 1) * LANES] for h in range(2)]

    def tile(j, carry, masked):
        k0 = pl.multiple_of(j * tk, tk)
        v = v_ref[pl.ds(k0, tk), :]
        out = []
        for h in range(2):
            k = k_ref[pl.ds(k0, tk), h * LANES:(h + 1) * LANES]
            s = _dot_nt(q[h], k)
            if masked:
                rows = i * tq + lax.broadcasted_iota(jnp.int32, (tq, tk), 0)
                cols = k0 + lax.broadcasted_iota(jnp.int32, (tq, tk), 1)
                s = jnp.where(cols <= rows, s, NEG_BIG)
            out.append(_online_softmax_step(s, v, *carry[h]))
        return tuple(out)

    init = tuple((jnp.full((tq, LANES), NEG_BIG, F32), jnp.zeros((tq, LANES), F32),
                  jnp.zeros((tq, LANES), F32)) for _ in range(2))
    nfull = i * (tq // tk)
    carry = lax.fori_loop(0, nfull, lambda j, c: tile(j, c, False), init)
    for d in range(tq // tk):
        carry = tile(nfull + d, carry, True)
    (_, l0, a0), (_, l1, a1) = carry
    lane = lax.broadcasted_iota(jnp.int32, (tq, LANES), 1)
    o_ref[...] = jnp.where(lane < dv, a0 / l0, a1 / l1)


def flash_causal_pairs(q, k, v):
    t = q.shape[0]
    nh = q.shape[1] // LANES
    dv = v.shape[1] // nh
    assert 2 * dv == LANES and nh % 2 == 0 and t % FLASH_TQ == 0 and FLASH_TQ % FLASH_TK == 0
    return pl.pallas_call(
        functools.partial(_flash_pair_body, tq=FLASH_TQ, tk=FLASH_TK, dv=dv),
        grid=(nh // 2, t // FLASH_TQ),
        in_specs=[
            pl.BlockSpec((FLASH_TQ, 2 * LANES), lambda hp, i: (i, hp)),
            pl.BlockSpec((t, 2 * LANES), lambda hp, i: (0, hp)),
            pl.BlockSpec((t, LANES), lambda hp, i: (0, hp)),
        ],
        out_specs=pl.BlockSpec((FLASH_TQ, LANES), lambda hp, i: (i, hp)),
        out_shape=jax.ShapeDtypeStruct((t, nh * dv), F32),
        compiler_params=pltpu.CompilerParams(
            dimension_semantics=("parallel", "arbitrary"),
            vmem_limit_bytes=V7X_VMEM_LIMIT_BYTES),
        name="mla_flash",
    )(q, k, v)


NSA_TQ = 256
NSA_TK = 512
NSA_SUPER = 64 * NSA_BLK


def _nsa_prompt_body(q_ref, g_ref, ka_ref, kw_ref, vs_ref, ck_ref, cv_ref, place_ref, o_ref, qaug_ref,
                     *, tq, tk, nbp):
    grp = NSA_GROUP
    rows = grp * tq
    i = pl.program_id(1)
    q0 = i * tq
    q4 = q_ref[0].reshape(rows, LANES)

    def row_pos(shape):
        r = lax.broadcasted_iota(jnp.int32, shape, 0)
        return q0 + (r & (tq - 1))

    s_c = _dot_nt(q4, ck_ref[0])
    jc = lax.broadcasted_iota(jnp.int32, (rows, nbp), 1)
    cmask = (jc + 1) * NSA_BLK <= row_pos((rows, nbp)) + 1
    s_c = jnp.where(cmask, s_c, NEG_BIG)
    e_c = jnp.where(cmask, jnp.exp(s_c - jnp.max(s_c, axis=1, keepdims=True)), 0.0)
    l_c = jnp.sum(e_c, axis=1, keepdims=True)
    p_c = e_c / jnp.maximum(l_c, 1e-30)
    o_c = jnp.dot(p_c.astype(BF16), cv_ref[0], preferred_element_type=F32)
    imp = p_c[0:tq]
    for g in range(1, grp):
        imp = imp + p_c[g * tq:(g + 1) * tq]

    j = lax.broadcasted_iota(jnp.int32, (tq, nbp), 1)
    cur = lax.shift_right_logical(q0 + lax.broadcasted_iota(jnp.int32, (tq, nbp), 0), 6)
    valid = j <= cur
    forced = (j == 0) | (j > cur - NSA_LOCAL)
    work = jnp.where(valid & jnp.logical_not(forced), imp, -1.0)
    for _ in range(NSA_TOPN - NSA_LOCAL - 1):
        idx = jnp.argmax(work, axis=1, keepdims=True)
        work = jnp.where(j == idx, -2.0, work)
    sel = valid & (forced | (work == -2.0) | (cur < NSA_TOPN))
    bias = jnp.where(sel, 0.0, NEG_BIG).astype(BF16)
    for st in range(nbp // 64):
        placed = jnp.dot(bias, place_ref[st], preferred_element_type=F32).astype(BF16)
        for g in range(grp):
            qaug_ref[st, g * tq:(g + 1) * tq, :] = q4[g * tq:(g + 1) * tq] + placed

    wlen = NSA_WINDOW + tq
    w0 = pl.multiple_of(jnp.maximum(q0 - NSA_WINDOW, 0), LANES)
    s_w = _dot_nt(q4, kw_ref[0, pl.ds(w0, wlen), :])
    dpos = row_pos((rows, wlen)) - (w0 + lax.broadcasted_iota(jnp.int32, (rows, wlen), 1))
    s_w = jnp.where((dpos >= 0) & (dpos <= NSA_WINDOW), s_w, NEG_BIG)
    e_w = jnp.exp(s_w - jnp.max(s_w, axis=1, keepdims=True))
    l_w = jnp.sum(e_w, axis=1, keepdims=True)
    a_w = jnp.dot(e_w.astype(BF16), vs_ref[0, pl.ds(w0, wlen), :], preferred_element_type=F32) / l_w

    def tile(jt, carry, masked):
        k0 = pl.multiple_of(jt * tk, tk)
        qa = qaug_ref[jt // (NSA_SUPER // tk)]
        s = _dot_nt(qa, ka_ref[0, pl.ds(k0, tk), :])
        if masked:
            kpos = k0 + lax.broadcasted_iota(jnp.int32, (rows, tk), 1)
            s = jnp.where(kpos <= row_pos((rows, tk)), s, NEG_BIG)
        return _online_softmax_step(s, vs_ref[0, pl.ds(k0, tk), :], *carry)

    init = (jnp.full((rows, LANES), NEG_BIG, F32), jnp.zeros((rows, LANES), F32), jnp.zeros((rows, LANES), F32))
    jd = q0 // tk
    carry = lax.fori_loop(0, jd, lambda jt, c: tile(jt, c, False), init)
    _, l_s, acc_s = tile(jd, carry, True)
    a_s = acc_s / l_s

    gt = g_ref[0].reshape(rows, 3)
    lane = lax.broadcasted_iota(jnp.int32, (rows, LANES), 1)
    mix = jnp.where(lane < NSA_DH, gt[:, 1:2] * a_s + gt[:, 0:1] * o_c, gt[:, 2:3] * a_w)
    both = mix + pltpu.roll(mix, NSA_DH, axis=1)
    lane_q = lax.broadcasted_iota(jnp.int32, (tq, LANES), 1)
    for pr in range(grp // 2):
        even = both[(2 * pr) * tq:(2 * pr + 1) * tq]
        odd = both[(2 * pr + 1) * tq:(2 * pr + 2) * tq]
        o_ref[:, pr * LANES:(pr + 1) * LANES] = jnp.where(lane_q < NSA_DH, even, odd)


def nsa_prompt_attention(qp, gates, ka, kw, vs, ck, cv):
    hkv, grp, t, _ = qp.shape
    nbp = ck.shape[1]
    tq, tk = NSA_TQ, NSA_TK
    assert t % tk == 0 and tk % tq == 0 and t >= NSA_WINDOW + tq and nbp % LANES == 0 and nbp * NSA_BLK >= t
    nsup = nbp // 64
    blk = np.arange(nbp)
    place = np.zeros((nsup, nbp, LANES), np.float32)
    place[blk // 64, blk, NSA_DH + blk % 64] = 1.0
    return pl.pallas_call(
        functools.partial(_nsa_prompt_body, tq=tq, tk=tk, nbp=nbp),
        grid=(hkv, t // tq),
        in_specs=[
            pl.BlockSpec((1, grp, tq, LANES), lambda h, i: (h, 0, i, 0)),
            pl.BlockSpec((1, grp, tq, 3), lambda h, i: (h, 0, i, 0)),
            pl.BlockSpec((1, t, LANES), lambda h, i: (h, 0, 0)),
            pl.BlockSpec((1, t, LANES), lambda h, i: (h, 0, 0)),
            pl.BlockSpec((1, t, LANES), lambda h, i: (h, 0, 0)),
            pl.BlockSpec((1, nbp, LANES), lambda h, i: (h, 0, 0)),
            pl.BlockSpec((1, nbp, LANES), lambda h, i: (h, 0, 0)),
            pl.BlockSpec((nsup, nbp, LANES), lambda h, i: (0, 0, 0)),
        ],
        out_specs=pl.BlockSpec((tq, grp * NSA_DH), lambda h, i: (i, h)),
        out_shape=jax.ShapeDtypeStruct((t, hkv * grp * NSA_DH), F32),
        scratch_shapes=[pltpu.VMEM((nsup, grp * tq, LANES), BF16)],
        compiler_params=pltpu.CompilerParams(
            dimension_semantics=("parallel", "arbitrary"),
            vmem_limit_bytes=V7X_VMEM_LIMIT_BYTES),
        name="nsa_prompt",
    )(qp, gates, ka, kw, vs, ck, cv, jnp.asarray(place, BF16))


def nsa_prompt_pallas(q, kv, gates):
    n, _, _, t, _ = q.shape
    assert n == 1
    nb = t // NSA_BLK
    nbp = -(-nb // LANES) * LANES
    zpad = jnp.zeros((NSA_KV_HEADS, t, LANES - NSA_DH), F32)
    qp = jnp.concatenate([q[0] * NSA_SCALE, jnp.zeros(q.shape[1:4] + (LANES - NSA_DH,), F32)], axis=-1).astype(BF16)
    code = (jnp.arange(t)[:, None] // NSA_BLK) % 64 == jnp.arange(64)[None, :]
    ka = jnp.concatenate([kv[0, 1, 0], jnp.broadcast_to(code.astype(F32), (NSA_KV_HEADS, t, 64))], axis=-1).astype(BF16)
    kw = jnp.concatenate([kv[0, 2, 0], zpad], axis=-1).astype(BF16)
    vs = jnp.concatenate([kv[0, 1, 1], kv[0, 2, 1]], axis=-1).astype(BF16)
    means = kv[0, 0].reshape(2, NSA_KV_HEADS, nb, NSA_BLK, NSA_DH).mean(axis=3, dtype=F32)
    means = jnp.pad(means, ((0, 0), (0, 0), (0, nbp - nb), (0, LANES - NSA_DH))).astype(BF16)
    o = nsa_prompt_attention(qp, gates[0], ka, kw, vs, means[0], means[1])
    return o.reshape(n, t, NSA_HEADS * NSA_DH), kv[:, 2, :, :, t - min(NSA_WINDOW, t):]


MLA_DEC_PAGES = 16


def _mla_decode_body(pt_ref, qlat_ref, qpe_ref, cnew_ref, kpenew_ref, ksnew_ref, wuv_ref, *refs, npg):
    ckv_refs, kpe_refs, ks_refs = refs[0:npg], refs[npg:2 * npg], refs[2 * npg:3 * npg]
    o_ref, m_ref, l_ref, acc_ref = refs[3 * npg:]
    g = pl.program_id(1)

    @pl.when(g == 0)
    def _():
        m_ref[...] = jnp.full(m_ref.shape, NEG_BIG, F32)
        l_ref[...] = jnp.zeros(l_ref.shape, F32)
        acc_ref[...] = jnp.zeros(acc_ref.shape, F32)

    qlat = qlat_ref[0].astype(BF16)
    qpe = qpe_ref[0].astype(BF16)
    scs, cs = [], []
    for s in range(npg):
        c = ckv_refs[s][0].astype(BF16)
        content = _dot_nt(qlat, c)
        pe = jnp.dot(qpe, kpe_refs[s][0].astype(BF16), preferred_element_type=F32)
        scs.append((content * ks_refs[s][0] + pe) * MLA_SCALE)
        cs.append(c)
    sall = jnp.concatenate(scs, axis=1)
    m_old = m_ref[...]
    m_new = jnp.maximum(m_old, jnp.max(sall, axis=1, keepdims=True))
    alpha = jnp.exp(m_old - m_new)
    p32 = jnp.exp(sall - pltpu.repeat(m_new, npg, axis=1))
    l_new = alpha * l_ref[...] + jnp.sum(p32, axis=1, keepdims=True)
    p = p32.astype(BF16)
    pv = jnp.dot(p[:, 0:PAGE_SIZE], cs[0], preferred_element_type=F32)
    for s in range(1, npg):
        pv = pv + jnp.dot(p[:, s * PAGE_SIZE:(s + 1) * PAGE_SIZE], cs[s], preferred_element_type=F32)
    acc_new = pltpu.repeat(alpha, MLA_KV_LORA // LANES, axis=1) * acc_ref[...] + pv
    m_ref[...] = m_new
    l_ref[...] = l_new
    acc_ref[...] = acc_new

    @pl.when(g == pl.num_programs(1) - 1)
    def _():
        cn = _bf16_round(cnew_ref[0])
        content = jnp.sum(_bf16_round(qlat_ref[0]) * cn, axis=1, keepdims=True)
        pe = jnp.sum(_bf16_round(qpe_ref[0]) * _bf16_round(kpenew_ref[0]), axis=1, keepdims=True)
        sc = (content * _col_from_row(ksnew_ref[0]) + pe) * MLA_SCALE
        m_fin = jnp.maximum(m_new, sc)
        a = jnp.exp(m_new - m_fin)
        pn = jnp.exp(sc - m_fin)
        l_fin = a * l_new + pn
        acc = pltpu.repeat(a, MLA_KV_LORA // LANES, axis=1) * acc_new + _bf16_round(pn[:, 0:1]) * cn
        o_lat = acc / pltpu.repeat(l_fin, MLA_KV_LORA // LANES, axis=1)
        full = jnp.dot(o_lat.astype(BF16), wuv_ref[...], preferred_element_type=F32)
        r = lax.broadcasted_iota(jnp.int32, full.shape, 0)
        cidx = lax.broadcasted_iota(jnp.int32, full.shape, 1)
        o_ref[0] = jnp.sum(jnp.where(r == cidx // MLA_V, full, 0.0), axis=0, keepdims=True)


def mla_decode(q_lat, q_pe, c_new, kpe_new, ks_new, w_uv, cache_ckv, cache_kpe_t, cache_ks_t, page_table):
    n, nh, cdim = q_lat.shape
    rdim = q_pe.shape[2]
    npages = page_table.shape[1]
    npg = MLA_DEC_PAGES
    assert npages % npg == 0
    per_n3 = lambda i, g, pt: (i, 0, 0)

    def page_map(s):
        return lambda i, g, pt: (pt[i, g * npg + s], 0, 0)

    in_specs = [
        pl.BlockSpec((1, nh, cdim), per_n3),
        pl.BlockSpec((1, nh, rdim), per_n3),
        pl.BlockSpec((1, 1, cdim), per_n3),
        pl.BlockSpec((1, 1, rdim), per_n3),
        pl.BlockSpec((1, 1, nh), per_n3),
        pl.BlockSpec(w_uv.shape, lambda i, g, pt: (0, 0)),
    ]
    in_specs += [pl.BlockSpec((1, PAGE_SIZE, cdim), page_map(s)) for s in range(npg)]
    in_specs += [pl.BlockSpec((1, rdim, PAGE_SIZE), page_map(s)) for s in range(npg)]
    in_specs += [pl.BlockSpec((1, nh, PAGE_SIZE), page_map(s)) for s in range(npg)]
    grid_spec = pltpu.PrefetchScalarGridSpec(
        num_scalar_prefetch=1, grid=(n, npages // npg), in_specs=in_specs,
        out_specs=pl.BlockSpec((1, 1, w_uv.shape[1]), per_n3),
        scratch_shapes=[pltpu.VMEM((nh, LANES), F32), pltpu.VMEM((nh, LANES), F32), pltpu.VMEM((nh, cdim), F32)])
    return pl.pallas_call(
        functools.partial(_mla_decode_body, npg=npg),
        grid_spec=grid_spec,
        out_shape=jax.ShapeDtypeStruct((n, 1, w_uv.shape[1]), F32),
        compiler_params=pltpu.CompilerParams(
            dimension_semantics=("parallel", "arbitrary"),
            vmem_limit_bytes=V7X_VMEM_LIMIT_BYTES),
        name="mla_decode",
    )(page_table, q_lat, q_pe, c_new, kpe_new, ks_new, w_uv,
      *([cache_ckv] * npg), *([cache_kpe_t] * npg), *([cache_ks_t] * npg))


NSA_DEC_PAGES = 16
NSA_DEC_ROWS = 8
NSA_DEC_PICK = NSA_TOPN - NSA_LOCAL - 1
NSA_DEC_NSEL = NSA_TOPN - 1


def _nsa_dec_select_body(pt_ref, q_ref, place_ref, *refs, npg, ngrp):
    page_refs = refs[:npg]
    oc_ref, idx_ref, parts_ref = refs[npg:]
    g = pl.program_id(1)
    bpp = PAGE_SIZE // NSA_BLK
    lane = lax.broadcasted_iota(jnp.int32, (NSA_DH, LANES), 1)
    blks = [jnp.zeros((NSA_DH, LANES), F32) for _ in range(4)]
    for s in range(npg):
        for kv in range(2):
            for h in range(NSA_KV_HEADS):
                slab = page_refs[s][0, kv, h]
                for b in range(bpp):
                    inblk = (lane >= b * NSA_BLK) & (lane < (b + 1) * NSA_BLK)
                    mean = jnp.sum(jnp.where(inblk, slab, 0.0), axis=1, keepdims=True) * (1.0 / NSA_BLK)
                    t = kv * NSA_KV_HEADS + h
                    blks[t] = jnp.where(lane == s * bpp + b, mean, blks[t])
    for t in range(4):
        parts_ref[g, t] = blks[t]

    @pl.when(g == ngrp - 1)
    def _():
        nb = ngrp * npg * bpp
        j = lax.broadcasted_iota(jnp.int32, (1, nb), 1)
        slot = lax.broadcasted_iota(jnp.int32, (1, LANES), 1)
        row = lax.broadcasted_iota(jnp.int32, (NSA_DEC_ROWS, nb), 0)
        for h in range(NSA_KV_HEADS):
            ck_t = jnp.zeros((NSA_DH, nb), F32)
            cv_t = jnp.zeros((NSA_DH, nb), F32)
            for gg in range(ngrp):
                ck_t = ck_t + jnp.dot(parts_ref[gg, h].astype(BF16), place_ref[gg], preferred_element_type=F32)
                cv_t = cv_t + jnp.dot(parts_ref[gg, NSA_KV_HEADS + h].astype(BF16), place_ref[gg],
                                      preferred_element_type=F32)
            q = (q_ref[0, h] * NSA_SCALE).astype(BF16)
            s_c = jnp.dot(q, ck_t.astype(BF16), preferred_element_type=F32)
            e_c = jnp.exp(s_c - jnp.max(s_c, axis=1, keepdims=True))
            p_c = e_c / jnp.sum(e_c, axis=1, keepdims=True)
            oc_ref[0, h] = _dot_nt(p_c.astype(BF16), cv_t.astype(BF16))
            imp = jnp.sum(jnp.where(row < NSA_GROUP, p_c, 0.0), axis=0, keepdims=True)
            work = jnp.where((j >= 1) & (j <= nb - NSA_LOCAL), imp, -1.0)
            picks = jnp.where(slot == NSA_DEC_PICK + 1, nb - 1, 0)
            for r in range(NSA_DEC_PICK):
                idx = jnp.argmax(work, axis=1, keepdims=True)
                work = jnp.where(j == idx, -2.0, work)
                picks = jnp.where(slot == r, idx, picks)
            idx_ref[0, h:h + 1, :] = picks


def nsa_decode_select(q8, cache_cmp_t, page_table):
    n = q8.shape[0]
    npages = page_table.shape[1]
    npg = NSA_DEC_PAGES
    bpp = PAGE_SIZE // NSA_BLK
    assert npages % npg == 0 and npg * bpp <= LANES
    ngrp = npages // npg
    nb = npages * bpp
    assert nb - NSA_LOCAL >= NSA_DEC_PICK + 1
    place = np.zeros((ngrp, LANES, nb), np.float32)
    c = np.arange(npg * bpp)
    for gg in range(ngrp):
        place[gg, c, gg * npg * bpp + c] = 1.0

    def page_map(s):
        return lambda i, g, pt: (pt[i, g * npg + s], 0, 0, 0, 0)

    in_specs = [pl.BlockSpec((1, NSA_KV_HEADS, NSA_DEC_ROWS, NSA_DH), lambda i, g, pt: (i, 0, 0, 0)),
                pl.BlockSpec((ngrp, LANES, nb), lambda i, g, pt: (0, 0, 0))]
    in_specs += [pl.BlockSpec((1, 2, NSA_KV_HEADS, NSA_DH, PAGE_SIZE), page_map(s)) for s in range(npg)]
    grid_spec = pltpu.PrefetchScalarGridSpec(
        num_scalar_prefetch=1, grid=(n, ngrp), in_specs=in_specs,
        out_specs=[pl.BlockSpec((1, NSA_KV_HEADS, NSA_DEC_ROWS, NSA_DH), lambda i, g, pt: (i, 0, 0, 0)),
                   pl.BlockSpec((1, NSA_KV_HEADS, LANES), lambda i, g, pt: (i, 0, 0))],
        scratch_shapes=[pltpu.VMEM((ngrp, 2 * NSA_KV_HEADS, NSA_DH, LANES), F32)])
    return pl.pallas_call(
        functools.partial(_nsa_dec_select_body, npg=npg, ngrp=ngrp),
        grid_spec=grid_spec,
        out_shape=[jax.ShapeDtypeStruct((n, NSA_KV_HEADS, NSA_DEC_ROWS, NSA_DH), F32),
                   jax.ShapeDtypeStruct((n, NSA_KV_HEADS, LANES), jnp.int32)],
        compiler_params=pltpu.CompilerParams(
            dimension_semantics=("parallel", "arbitrary"),
            vmem_limit_bytes=V7X_VMEM_LIMIT_BYTES),
        name="nsa_decode_select",
    )(page_table, q8, jnp.asarray(place, BF16), *([cache_cmp_t] * npg))


def _nsa_dec_attend_body(pg_ref, half_ref, q_ref, gate_ref, oc_ref, new_ref, *refs, nsel):
    sel_refs = refs[:nsel]
    win_ref, o_ref, wout_ref = refs[nsel:]
    i, h = pl.program_id(0), pl.program_id(1)
    qf = _bf16_round(q_ref[0, 0] * NSA_SCALE)
    q = qf.astype(BF16)
    new = new_ref[0, 0]
    lane = lax.broadcasted_iota(jnp.int32, (NSA_DEC_ROWS, PAGE_SIZE), 1)

    def attend(score_tiles, value_tiles, k_new, v_new):
        s_new = jnp.sum(qf * _bf16_round(k_new), axis=1, keepdims=True)
        sall = jnp.concatenate(score_tiles, axis=1) if len(score_tiles) > 1 else score_tiles[0]
        m = jnp.maximum(jnp.max(sall, axis=1, keepdims=True), s_new)
        p = jnp.exp(sall - m)
        p_new = jnp.exp(s_new - m)
        l = jnp.sum(p, axis=1, keepdims=True) + p_new
        pb = p.astype(BF16)
        acc = _bf16_round(p_new) * _bf16_round(v_new)
        off = 0
        for vt in value_tiles:
            w = vt.shape[1]
            acc = acc + _dot_nt(pb[:, off:off + w], vt)
            off += w
        return acc / l

    s_tiles, v_tiles = [], []
    for b in range(nsel):
        hb = half_ref[(i * NSA_KV_HEADS + h) * nsel + b]
        sb = jnp.dot(q, sel_refs[b][0, 0, 0].astype(BF16), preferred_element_type=F32)
        s_tiles.append(jnp.where(lax.shift_right_logical(lane, 6) == hb, sb, NEG_BIG))
        v_tiles.append(sel_refs[b][0, 1, 0].astype(BF16))
    o_s = attend(s_tiles, v_tiles, new[0:1], new[1:2])
    s_w = jnp.dot(q, win_ref[0, 0, 0].astype(BF16), preferred_element_type=F32)
    o_w = attend([s_w], [win_ref[0, 1, 0].astype(BF16)], new[2:3], new[3:4])
    gt = gate_ref[0, 0]
    o_ref[0, 0] = gt[:, 0:1] * oc_ref[0, 0] + gt[:, 1:2] * o_s + gt[:, 2:3] * o_w

    wl = win_ref.shape[4]
    wlane = lax.broadcasted_iota(jnp.int32, (NSA_DH, wl), 1)
    for kv in range(2):
        col = _col_from_row(new[2 + kv:3 + kv])
        wout_ref[0, kv, 0] = jnp.where(wlane == wl - 1, col, pltpu.roll(win_ref[0, kv, 0], wl - 1, axis=1))


def nsa_decode_attend(pages, halves, q8, gates8, oc8, new_rows, cache_sel_t, win_t):
    n = q8.shape[0]
    nsel = NSA_DEC_NSEL
    wl = win_t.shape[4]
    per = lambda i, h, pg, hf: (i, h, 0, 0)

    def sel_map(b):
        return lambda i, h, pg, hf: (pg[(i * NSA_KV_HEADS + h) * nsel + b], 0, h, 0, 0)

    in_specs = [pl.BlockSpec((1, 1, NSA_DEC_ROWS, NSA_DH), per),
                pl.BlockSpec((1, 1, NSA_DEC_ROWS, 3), per),
                pl.BlockSpec((1, 1, NSA_DEC_ROWS, NSA_DH), per),
                pl.BlockSpec((1, 1, 4, NSA_DH), per)]
    in_specs += [pl.BlockSpec((1, 2, 1, NSA_DH, PAGE_SIZE), sel_map(b)) for b in range(nsel)]
    in_specs += [pl.BlockSpec((1, 2, 1, NSA_DH, wl), lambda i, h, pg, hf: (i, 0, h, 0, 0))]
    grid_spec = pltpu.PrefetchScalarGridSpec(
        num_scalar_prefetch=2, grid=(n, NSA_KV_HEADS), in_specs=in_specs,
        out_specs=[pl.BlockSpec((1, 1, NSA_DEC_ROWS, NSA_DH), per),
                   pl.BlockSpec((1, 2, 1, NSA_DH, wl), lambda i, h, pg, hf: (i, 0, h, 0, 0))])
    return pl.pallas_call(
        functools.partial(_nsa_dec_attend_body, nsel=nsel),
        grid_spec=grid_spec,
        out_shape=[jax.ShapeDtypeStruct((n, NSA_KV_HEADS, NSA_DEC_ROWS, NSA_DH), F32),
                   jax.ShapeDtypeStruct(win_t.shape, F32)],
        compiler_params=pltpu.CompilerParams(
            dimension_semantics=("parallel", "arbitrary"),
            vmem_limit_bytes=V7X_VMEM_LIMIT_BYTES),
        name="nsa_decode_attend",
    )(pages, halves, q8, gates8, oc8, new_rows, *([cache_sel_t] * nsel), win_t)


def nsa_sample_pallas(q, kv, gates, cache_cmp, cache_sel, win_state, page_table):
    n = q.shape[0]
    assert q.shape[3] == 1 and win_state.shape[3] == NSA_WINDOW
    bpp = PAGE_SIZE // NSA_BLK
    padrows = ((0, 0), (0, 0), (0, NSA_DEC_ROWS - NSA_GROUP), (0, 0))
    q8 = jnp.pad(q[:, :, :, 0], padrows)
    gates8 = jnp.pad(gates[:, :, :, 0], padrows)
    feature_major = (0, 1, 2, 4, 3)
    oc8, picks = nsa_decode_select(q8, jnp.transpose(cache_cmp, feature_major), page_table)
    idx = picks[:, :, :NSA_DEC_NSEL]
    pages = jnp.take_along_axis(page_table[:, None, :], idx // bpp, axis=2)
    new_rows = jnp.stack([kv[:, 1, 0, :, 0], kv[:, 1, 1, :, 0], kv[:, 2, 0, :, 0], kv[:, 2, 1, :, 0]], axis=2)
    o8, win_t = nsa_decode_attend(pages.reshape(-1), (idx % bpp).reshape(-1), q8, gates8, oc8, new_rows,
                                  jnp.transpose(cache_sel, feature_major), jnp.transpose(win_state, feature_major))
    o = o8[:, :, :NSA_GROUP].reshape(n, 1, NSA_HEADS * NSA_DH)
    return o, jnp.transpose(win_t, feature_major)


def split_cols(x, sizes):
    return jnp.split(x, np.cumsum(sizes)[:-1].tolist(), axis=-1)


def rms_norm(x, g):
    xf = x.astype(F32)
    y = xf * lax.rsqrt(jnp.mean(xf * xf, axis=-1, keepdims=True) + EPS)
    return (y * g.astype(F32)).astype(x.dtype)


def layer_norm(x, g, b):
    xf = x.astype(F32)
    xc = xf - jnp.mean(xf, axis=-1, keepdims=True)
    y = xc * lax.rsqrt(jnp.mean(xc * xc, axis=-1, keepdims=True) + EPS)
    return (y * g.astype(F32) + b.astype(F32)).astype(x.dtype)


def l2_normalize(x):
    xf = x.astype(F32)
    return xf * lax.rsqrt(jnp.sum(xf * xf, axis=-1, keepdims=True) + EPS)


def rope(x, pos):
    d = x.shape[-1]
    inv = ROPE_THETA ** (-jnp.arange(0, d, 2, dtype=F32) / d)
    ang = pos.astype(F32)[..., None] * inv
    cos, sin = jnp.cos(ang), jnp.sin(ang)
    xf = x.astype(F32)
    x1, x2 = xf[..., : d // 2], xf[..., d // 2:]
    return jnp.concatenate([x1 * cos - x2 * sin, x2 * cos + x1 * sin], axis=-1).astype(x.dtype)


def partial_rope(x, pos, rot):
    return jnp.concatenate([rope(x[..., :rot], pos), x[..., rot:]], axis=-1)


def causal_conv(x_ext, w):
    t = x_ext.shape[1] - (GDN_CONV - 1)
    return sum(x_ext[:, i:i + t] * w[i] for i in range(GDN_CONV))


def gdn_prepare(conv_out, b_raw, a_raw, a_log, dt_bias):
    n, t, _ = conv_out.shape
    q, k, v = split_cols(conv_out, (GDN_HEADS * GDN_DK, GDN_HEADS * GDN_DK, GDN_HEADS * GDN_DV))
    q = l2_normalize(q.reshape(n, t, GDN_HEADS, GDN_DK)) * (GDN_DK ** -0.5)
    k = l2_normalize(k.reshape(n, t, GDN_HEADS, GDN_DK))
    v = v.reshape(n, t, GDN_HEADS, GDN_DV).astype(F32)
    beta = jax.nn.sigmoid(b_raw.astype(F32))
    g = -jnp.exp(a_log.astype(F32)) * jax.nn.softplus(a_raw.astype(F32) + dt_bias.astype(F32))
    return q, k, v, beta, g


def gdn_chunked(q, k, v, beta, g):
    n, t, h, _ = q.shape
    c = GDN_CHUNK
    nc = t // c

    def chunks(x):
        return jnp.moveaxis(x.reshape((n, nc, c, h) + x.shape[3:]), 3, 1)

    q, k, v, beta, g = map(chunks, (q, k, v, beta, g))
    gc = jnp.cumsum(g, axis=-1)
    tril = jnp.tril(jnp.ones((c, c), bool))
    stril = jnp.tril(jnp.ones((c, c), bool), -1)
    diff = gc[..., :, None] - gc[..., None, :]
    decay = jnp.where(tril, jnp.exp(jnp.where(tril, diff, 0.0)), 0.0)
    kb = k * beta[..., None]
    m = jnp.where(stril, jnp.einsum('nhcid,nhcjd->nhcij', kb, k) * decay, 0.0)
    eye = jnp.eye(c, dtype=F32)
    tinv = lax.linalg.triangular_solve(eye + m, jnp.broadcast_to(eye, m.shape),
                                       left_side=True, lower=True, unit_diagonal=True)
    u = tinv @ (v * beta[..., None])
    w = tinv @ (kb * jnp.exp(gc)[..., None])
    attn = jnp.where(tril, jnp.einsum('nhcid,nhcjd->nhcij', q, k) * decay, 0.0)
    qg = q * jnp.exp(gc)[..., None]
    kd = k * jnp.exp(gc[..., -1:] - gc)[..., None]
    gl = jnp.exp(gc[..., -1])

    def step(s, xs):
        u_c, w_c, a_c, qg_c, kd_c, gl_c = xs
        v_new = u_c - jnp.einsum('nhcd,nhde->nhce', w_c, s)
        o_c = jnp.einsum('nhcd,nhde->nhce', qg_c, s) + jnp.einsum('nhij,nhje->nhie', a_c, v_new)
        s = s * gl_c[..., None, None] + jnp.einsum('nhcd,nhce->nhde', kd_c, v_new)
        return s, o_c

    xs = tuple(jnp.moveaxis(x, 2, 0) for x in (u, w, attn, qg, kd, gl))
    s0 = jnp.zeros((n, h, GDN_DK, GDN_DV), F32)
    s_fin, o = lax.scan(step, s0, xs)
    o = jnp.transpose(o, (1, 0, 3, 2, 4)).reshape(n, t, h, GDN_DV)
    return o, s_fin


def gdn_recurrent(q, k, v, beta, g, s0):
    def step(s, xs):
        q_t, k_t, v_t, b_t, g_t = xs
        s = s * jnp.exp(g_t)[..., None, None]
        delta = (v_t - jnp.einsum('nhde,nhd->nhe', s, k_t)) * b_t[..., None]
        s = s + jnp.einsum('nhd,nhe->nhde', k_t, delta)
        return s, jnp.einsum('nhde,nhd->nhe', s, q_t)

    xs = tuple(jnp.moveaxis(x, 1, 0) for x in (q, k, v, beta, g))
    s_fin, o = lax.scan(step, s0, xs)
    return jnp.moveaxis(o, 0, 1), s_fin


def gdn_output(o, z, out_norm):
    n, t = z.shape[:2]
    gate = jax.nn.silu(z.reshape(n, t, GDN_HEADS, GDN_DV).astype(F32))
    return (rms_norm(o, out_norm) * gate).astype(z.dtype).reshape(n, t, GDN_HEADS * GDN_DV)


def even_project(xn, pos, w_in, q_norm, k_norm):
    n, t, _ = xn.shape
    qkv, z, b_raw, a_raw, q, kv, g_raw = split_cols(xn @ w_in, EVEN_SIZES)
    q = partial_rope(rms_norm(q.reshape(n, t, NSA_HEADS, NSA_DH), q_norm), pos[:, None], NSA_ROT)
    q = jnp.transpose(q.reshape(n, t, NSA_KV_HEADS, NSA_GROUP, NSA_DH), (0, 2, 3, 1, 4))
    kv = kv.reshape(n, t, 3, 2, NSA_KV_HEADS, NSA_DH)
    k = partial_rope(rms_norm(kv[:, :, :, 0], k_norm[:, None, :]), pos[:, None, None], NSA_ROT)
    kv = jnp.transpose(jnp.stack([k, kv[:, :, :, 1]], axis=3), (0, 2, 3, 4, 1, 5))
    gates = jax.nn.sigmoid(g_raw.astype(F32)).reshape(n, t, NSA_KV_HEADS, NSA_GROUP, 3)
    return qkv, z, b_raw, a_raw, q, kv, jnp.transpose(gates, (0, 2, 3, 1, 4))


def even_prompt(xn, pos, w_in, w_out, conv_w, a_log, dt_bias, out_norm, q_norm, k_norm):
    n = xn.shape[0]
    qkv, z, b_raw, a_raw, q, kv, gates = even_project(xn, pos, w_in, q_norm, k_norm)
    qkv_ext = jnp.concatenate([jnp.zeros((n, GDN_CONV - 1, GDN_CONV_CH), qkv.dtype), qkv], axis=1)
    gq, gk, gv, beta, g = gdn_prepare(jax.nn.silu(causal_conv(qkv_ext, conv_w)), b_raw, a_raw, a_log, dt_bias)
    o, s_fin = gdn_chunked(gq, gk, gv, beta, g)
    nsa_out, win = nsa_prompt_pallas(q, kv, gates)
    y = jnp.concatenate([gdn_output(o, z, out_norm), nsa_out], axis=-1) @ w_out
    return y, (s_fin.astype(xn.dtype), qkv_ext[:, -(GDN_CONV - 1):], kv[:, 0], kv[:, 1], win)


def even_sample(xn, pos, gdn_state, conv_state, cache_cmp, cache_sel, win_state, page_table,
                w_in, w_out, conv_w, a_log, dt_bias, out_norm, q_norm, k_norm):
    qkv, z, b_raw, a_raw, q, kv, gates = even_project(xn, pos, w_in, q_norm, k_norm)
    qkv_ext = jnp.concatenate([conv_state.astype(qkv.dtype), qkv], axis=1)
    gq, gk, gv, beta, g = gdn_prepare(jax.nn.silu(causal_conv(qkv_ext, conv_w)), b_raw, a_raw, a_log, dt_bias)
    o, s_fin = gdn_recurrent(gq, gk, gv, beta, g, gdn_state.astype(F32))
    nsa_out, win = nsa_sample_pallas(q, kv, gates, cache_cmp, cache_sel, win_state, page_table)
    y = jnp.concatenate([gdn_output(o, z, out_norm), nsa_out], axis=-1) @ w_out
    return y, (s_fin.astype(xn.dtype), qkv_ext[:, -(GDN_CONV - 1):], kv[:, 0], kv[:, 1], win)


def sgu_mix(u, v, w_s, b_s):
    n, t, _ = u.shape
    l = min(t, SGU_CHUNK)
    nc = t // l
    w = jnp.tril(w_s[:, :l, :l])
    mix = jnp.einsum('gts,ncsgd->nctgd', w, v.reshape(n, nc, l, SGU_GROUPS, SGU_DG))
    mix = mix + jnp.transpose(b_s[:, :l])[:, :, None]
    return (u.reshape(n, nc, l, SGU_GROUPS, SGU_DG) * mix).reshape(n, t, SGU_WIDTH)


def odd_project(xn, pos, w_in, ln_g, ln_b, cq_norm, ckv_norm, w_uq, w_uk, qn_norm, qr_norm, kr_norm):
    n, t, _ = xn.shape
    uv, cq, ckv, kr = split_cols(xn @ w_in, ODD_SIZES)
    uv = jax.nn.gelu(uv, approximate=False)
    u = uv[..., :SGU_WIDTH]
    v = layer_norm(uv[..., SGU_WIDTH:], ln_g, ln_b)
    q = (rms_norm(cq, cq_norm) @ w_uq).reshape(n, t, MLA_HEADS, MLA_NOPE + MLA_ROPE)
    q_nope = rms_norm(q[..., :MLA_NOPE], qn_norm)
    q_pe = rope(rms_norm(q[..., MLA_NOPE:], qr_norm), pos[:, None])
    c = rms_norm(ckv, ckv_norm)
    k_pe = rope(rms_norm(kr, kr_norm), pos)
    k_nope = jnp.einsum('ntc,chd->nthd', c, w_uk).astype(F32)
    kscale = lax.rsqrt(jnp.mean(k_nope * k_nope, axis=-1) + EPS)
    return u, v, q_nope, q_pe, c, k_pe, k_nope, kscale


def odd_prompt(xn, pos, w_in, w_out, ln_g, ln_b, sgu_w, sgu_b, cq_norm, ckv_norm, w_uq, w_uk, w_uv,
               qn_norm, qr_norm, kn_norm, kr_norm):
    n, t, _ = xn.shape
    u, v, q_nope, q_pe, c, k_pe, k_nope, kscale = odd_project(
        xn, pos, w_in, ln_g, ln_b, cq_norm, ckv_norm, w_uq, w_uk, qn_norm, qr_norm, kr_norm)
    sgu = sgu_mix(u, v, sgu_w, sgu_b)
    k_nope_n = (k_nope * kscale[..., None] * kn_norm.astype(F32)).astype(xn.dtype)
    pad = jnp.zeros((n, t, MLA_HEADS, LANES - MLA_NOPE - MLA_ROPE), F32)
    q = jnp.concatenate([q_nope * MLA_SCALE, q_pe * MLA_SCALE, pad], axis=-1)
    k = jnp.concatenate([k_nope_n, jnp.broadcast_to(k_pe[:, :, None, :], (n, t, MLA_HEADS, MLA_ROPE)), pad], axis=-1)
    vv = jnp.einsum('ntc,chd->nthd', c, w_uv)
    assert n == 1
    att = flash_causal_pairs(q.reshape(t, MLA_HEADS * LANES).astype(BF16),
                             k.reshape(t, MLA_HEADS * LANES).astype(BF16),
                             vv.reshape(t, MLA_HEADS * MLA_V).astype(BF16)).reshape(n, t, MLA_HEADS * MLA_V)
    y = jnp.concatenate([sgu, att], axis=-1) @ w_out
    return y, (c, k_pe, kscale.astype(xn.dtype))


def odd_sample(xn, pos, cache_ckv, cache_kpe, cache_kscale, page_table, w_in, w_out, ln_g, ln_b, sgu_w,
               sgu_b, cq_norm, ckv_norm, w_uq, w_uk, w_uv, qn_norm, qr_norm, kn_norm, kr_norm):
    n, s, _ = xn.shape
    u, v, q_nope, q_pe, c, k_pe, k_nope, kscale = odd_project(
        xn, pos, w_in, ln_g, ln_b, cq_norm, ckv_norm, w_uq, w_uk, qn_norm, qr_norm, kr_norm)
    sgu = sgu_mix(u, v, sgu_w, sgu_b)
    q_lat = jnp.einsum('nqhd,chd->nqhc', q_nope.astype(F32) * kn_norm.astype(F32), w_uk)
    assert s == 1
    att = mla_decode(q_lat[:, 0], q_pe[:, 0], c, k_pe, kscale.astype(F32),
                     w_uv.reshape(MLA_KV_LORA, MLA_HEADS * MLA_V).astype(BF16), cache_ckv,
                     jnp.swapaxes(cache_kpe, 1, 2), jnp.swapaxes(cache_kscale, 1, 2), page_table)
    y = jnp.concatenate([sgu, att], axis=-1) @ w_out
    return y, (c, k_pe, kscale.astype(xn.dtype), v)


def kernel(x_prompt, x_sample, state_gdn, state_gdn_conv, cache_nsa_cmp, cache_nsa_sel, state_nsa_win,
           cache_mla_ckv, cache_mla_kpe, cache_mla_kscale, page_table, norm_mix, norm_ffn, ffn_up, ffn_down,
           w_in_even, w_out_even, gdn_conv_w, gdn_a_log, gdn_dt_bias, gdn_out_norm, nsa_q_norm, nsa_k_norm,
           w_in_odd, w_out_odd, sgu_ln_g, sgu_ln_b, sgu_w, sgu_b, mla_cq_norm, mla_ckv_norm, mla_w_uq,
           mla_w_uk, mla_w_uv, mla_qn_norm, mla_qr_norm, mla_kn_norm, mla_kr_norm):
    pos_p = jnp.arange(x_prompt.shape[1], dtype=jnp.int32)
    pos_s = page_table.shape[1] * PAGE_SIZE + jnp.arange(x_sample.shape[1], dtype=jnp.int32)
    ffn_up_b = ffn_up.astype(BF16)
    ffn_down_b = ffn_down.astype(BF16)
    hp, hs = x_prompt, x_sample
    for layer in range(norm_mix.shape[0]):
        xp = rms_norm(hp, norm_mix[layer])
        xs = rms_norm(hs, norm_mix[layer])
        if layer % 2 == 0:
            mp, (p_gdn_state, p_gdn_conv, p_nsa_cmp, p_nsa_sel, p_nsa_win) = even_prompt(
                xp, pos_p, w_in_even, w_out_even, gdn_conv_w, gdn_a_log, gdn_dt_bias, gdn_out_norm,
                nsa_q_norm, nsa_k_norm)
            ms, (s_gdn_state, s_gdn_conv, s_nsa_cmp, s_nsa_sel, s_nsa_win) = even_sample(
                xs, pos_s, state_gdn, state_gdn_conv, cache_nsa_cmp, cache_nsa_sel, state_nsa_win, page_table,
                w_in_even, w_out_even, gdn_conv_w, gdn_a_log, gdn_dt_bias, gdn_out_norm, nsa_q_norm, nsa_k_norm)
        else:
            mp, (p_mla_ckv, p_mla_kpe, p_mla_kscale) = odd_prompt(
                xp, pos_p, w_in_odd, w_out_odd, sgu_ln_g, sgu_ln_b, sgu_w, sgu_b, mla_cq_norm, mla_ckv_norm,
                mla_w_uq, mla_w_uk, mla_w_uv, mla_qn_norm, mla_qr_norm, mla_kn_norm, mla_kr_norm)
            ms, (s_mla_ckv, s_mla_kpe, s_mla_kscale, s_sgu_v) = odd_sample(
                xs, pos_s, cache_mla_ckv, cache_mla_kpe, cache_mla_kscale, page_table, w_in_odd, w_out_odd,
                sgu_ln_g, sgu_ln_b, sgu_w, sgu_b, mla_cq_norm, mla_ckv_norm, mla_w_uq, mla_w_uk, mla_w_uv,
                mla_qn_norm, mla_qr_norm, mla_kn_norm, mla_kr_norm)
        hp = sq_relu_mlp(hp + mp, norm_ffn[layer], ffn_up_b[layer], ffn_down_b[layer])
        hs = sq_relu_mlp(hs + ms, norm_ffn[layer], ffn_up_b[layer], ffn_down_b[layer])
    return (hp, hs, p_gdn_state, p_gdn_conv, p_nsa_cmp, p_nsa_sel, p_nsa_win, p_mla_ckv, p_mla_kpe, p_mla_kscale,
            s_gdn_state, s_gdn_conv, s_nsa_cmp, s_nsa_sel, s_nsa_win, s_mla_ckv, s_mla_kpe, s_mla_kscale, s_sgu_v)
```

```python
import functools

import jax
import jax.numpy as jnp
import numpy as np
from jax import lax
from jax.experimental import pallas as pl
from jax.experimental.pallas import tpu as pltpu

F32 = jnp.float32
BF16 = jnp.bfloat16

D_MODEL = 1024
PAGE_SIZE = 128
ROPE_THETA = 500000.0
EPS = 1e-6

GDN_HEADS = 4
GDN_DK = 128
GDN_DV = 128
GDN_CONV = 4
GDN_CHUNK = 64
GDN_CONV_CH = GDN_HEADS * (2 * GDN_DK + GDN_DV)

NSA_HEADS = 8
NSA_KV_HEADS = 2
NSA_GROUP = NSA_HEADS // NSA_KV_HEADS
NSA_DH = 64
NSA_ROT = NSA_DH // 4
NSA_BLK = 64
NSA_TOPN = 16
NSA_LOCAL = 2
NSA_WINDOW = 512
NSA_SCALE = NSA_DH ** -0.5

SGU_GROUPS = 4
SGU_DG = 128
SGU_CHUNK = 128
SGU_WIDTH = SGU_GROUPS * SGU_DG

MLA_HEADS = 8
MLA_Q_LORA = 256
MLA_KV_LORA = 256
MLA_NOPE = 64
MLA_ROPE = 32
MLA_V = 64
MLA_SCALE = (MLA_NOPE + MLA_ROPE) ** -0.5

EVEN_SIZES = (GDN_CONV_CH, GDN_HEADS * GDN_DV, GDN_HEADS, GDN_HEADS,
              NSA_HEADS * NSA_DH, 3 * 2 * NSA_KV_HEADS * NSA_DH, 3 * NSA_HEADS)
ODD_SIZES = (2 * SGU_WIDTH, MLA_Q_LORA, MLA_KV_LORA, MLA_ROPE)

V7X_VMEM_LIMIT_BYTES = 56 * 1024 * 1024
LANES = 128
NEG_BIG = -1e30


def _dot_nt(a, b):
    return lax.dot_general(a, b, (((1,), (1,)), ((), ())), preferred_element_type=F32)


def _bf16_round(x):
    return x.astype(BF16).astype(F32)


def _lane_tile(x, k):
    return x if k == 1 else jnp.concatenate([x] * k, axis=1)


def _col_from_row(row):
    k = row.shape[1]
    r = lax.broadcasted_iota(jnp.int32, (k, k), 0)
    c = lax.broadcasted_iota(jnp.int32, (k, k), 1)
    return jnp.sum(jnp.where(r == c, jnp.broadcast_to(row, (k, k)), 0.0), axis=1, keepdims=True)


def _mlp_body(x_ref, g_ref, wu_ref, wd_ref, o_ref, xn_ref, acc_ref):
    f = pl.program_id(1)

    @pl.when(f == 0)
    def _():
        x = x_ref[...]
        y = x * lax.rsqrt(jnp.mean(x * x, axis=-1, keepdims=True) + EPS)
        xn_ref[...] = (y * g_ref[...]).astype(BF16)
        acc_ref[...] = jnp.zeros_like(acc_ref)

    z = jnp.dot(xn_ref[...], wu_ref[...], preferred_element_type=F32)
    a = jnp.square(jnp.maximum(z, 0.0)).astype(BF16)
    acc_ref[...] += jnp.dot(a, wd_ref[...], preferred_element_type=F32)

    @pl.when(f == pl.num_programs(1) - 1)
    def _():
        o_ref[...] = x_ref[...] + acc_ref[...]


def mlp_tiles(t):
    tm = min(t, 1024)
    tf = 512
    return tm, tf


def sq_relu_mlp(h, g, w_up, w_down):
    n, t, d = h.shape
    rows = n * t
    x = h.reshape(rows, d)
    tm, tf = mlp_tiles(rows)
    dff = w_up.shape[1]
    out = pl.pallas_call(
        _mlp_body,
        grid=(rows // tm, dff // tf),
        in_specs=[
            pl.BlockSpec((tm, d), lambda i, f: (i, 0)),
            pl.BlockSpec((1, d), lambda i, f: (0, 0)),
            pl.BlockSpec((d, tf), lambda i, f: (0, f)),
            pl.BlockSpec((tf, d), lambda i, f: (f, 0)),
        ],
        out_specs=pl.BlockSpec((tm, d), lambda i, f: (i, 0)),
        out_shape=jax.ShapeDtypeStruct((rows, d), F32),
        scratch_shapes=[pltpu.VMEM((tm, d), BF16), pltpu.VMEM((tm, d), F32)],
        compiler_params=pltpu.CompilerParams(
            dimension_semantics=("parallel", "arbitrary"),
            vmem_limit_bytes=V7X_VMEM_LIMIT_BYTES),
        name="sq_relu_mlp",
    )(x, g.reshape(1, d), w_up, w_down)
    return out.reshape(n, t, d)


FLASH_TQ = 1024
FLASH_TK = 1024


def _online_softmax_step(s, v, m, l, acc):
    tk = s.shape[1]
    m_new = jnp.maximum(m, jnp.max(s, axis=1, keepdims=True))
    alpha = jnp.exp(m - m_new)
    p = jnp.exp(s - _lane_tile(m_new, tk // LANES))
    l_new = alpha * l + jnp.sum(p, axis=1, keepdims=True)
    pv = jnp.dot(p.astype(BF16), v, preferred_element_type=F32)
    acc_new = _lane_tile(alpha, acc.shape[1] // LANES) * acc + pv
    return m_new, l_new, acc_new


def _flash_pair_body(q_ref, k_ref, v_ref, o_ref, *, tq, tk, dv):
    i = pl.program_id(1)
    q = [q_ref[:, h * LANES:(h + 1) * LANES] for h in range(2)]

    def tile(j, carry, masked):
        k0 = pl.multiple_of(j * tk, tk)
        v = v_ref[pl.ds(k0, tk), :]
        out = []
        for h in range(2):
            k = k_ref[pl.ds(k0, tk), h * LANES:(h + 1) * LANES]
            s = _dot_nt(q[h], k)
            if masked:
                rows = i * tq + lax.broadcasted_iota(jnp.int32, (tq, tk), 0)
                cols = k0 + lax.broadcasted_iota(jnp.int32, (tq, tk), 1)
                s = jnp.where(cols <= rows, s, NEG_BIG)
            out.append(_online_softmax_step(s, v, *carry[h]))
        return tuple(out)

    init = tuple((jnp.full((tq, LANES), NEG_BIG, F32), jnp.zeros((tq, LANES), F32),
                  jnp.zeros((tq, LANES), F32)) for _ in range(2))
    nfull = i * (tq // tk)
    carry = lax.fori_loop(0, nfull, lambda j, c: tile(j, c, False), init)
    for d in range(tq // tk):
        carry = tile(nfull + d, carry, True)
    (_, l0, a0), (_, l1, a1) = carry
    lane = lax.broadcasted_iota(jnp.int32, (tq, LANES), 1)
    o_ref[...] = jnp.where(lane < dv, a0 / l0, a1 / l1)


def flash_causal_pairs(q, k, v):
    t = q.shape[0]
    nh = q.shape[1] // LANES
    dv = v.shape[1] // nh
    assert 2 * dv == LANES and nh % 2 == 0 and t % FLASH_TQ == 0 and FLASH_TQ % FLASH_TK == 0
    return pl.pallas_call(
        functools.partial(_flash_pair_body, tq=FLASH_TQ, tk=FLASH_TK, dv=dv),
        grid=(nh // 2, t // FLASH_TQ),
        in_specs=[
            pl.BlockSpec((FLASH_TQ, 2 * LANES), lambda hp, i: (i, hp)),
            pl.BlockSpec((t, 2 * LANES), lambda hp, i: (0, hp)),
            pl.BlockSpec((t, LANES), lambda hp, i: (0, hp)),
        ],
        out_specs=pl.BlockSpec((FLASH_TQ, LANES), lambda hp, i: (i, hp)),
        out_shape=jax.ShapeDtypeStruct((t, nh * dv), F32),
        compiler_params=pltpu.CompilerParams(
            dimension_semantics=("parallel", "arbitrary"),
            vmem_limit_bytes=V7X_VMEM_LIMIT_BYTES),
        name="mla_flash",
    )(q, k, v)


NSA_TQ = 256
NSA_TK = 1024
NSA_SUPER = 64 * NSA_BLK


def _nsa_prompt_body(q_ref, g_ref, ka_ref, kw_ref, vs_ref, ck_ref, cv_ref, place_ref, o_ref, qaug_ref,
                     *, tq, tk, nbp):
    grp = NSA_GROUP
    rows = grp * tq
    i = pl.program_id(1)
    q0 = i * tq
    q4 = q_ref[0].reshape(rows, LANES)

    def row_pos(shape):
        r = lax.broadcasted_iota(jnp.int32, shape, 0)
        return q0 + (r & (tq - 1))

    s_c = _dot_nt(q4, ck_ref[0])
    jc = lax.broadcasted_iota(jnp.int32, (rows, nbp), 1)
    cmask = (jc + 1) * NSA_BLK <= row_pos((rows, nbp)) + 1
    s_c = jnp.where(cmask, s_c, NEG_BIG)
    e_c = jnp.where(cmask, jnp.exp(s_c - jnp.max(s_c, axis=1, keepdims=True)), 0.0)
    l_c = jnp.sum(e_c, axis=1, keepdims=True)
    p_c = e_c / jnp.maximum(l_c, 1e-30)
    o_c = jnp.dot(p_c.astype(BF16), cv_ref[0], preferred_element_type=F32)
    imp = p_c[0:tq]
    for g in range(1, grp):
        imp = imp + p_c[g * tq:(g + 1) * tq]

    j = lax.broadcasted_iota(jnp.int32, (tq, nbp), 1)
    cur = lax.shift_right_logical(q0 + lax.broadcasted_iota(jnp.int32, (tq, nbp), 0), 6)
    valid = j <= cur
    forced = (j == 0) | (j > cur - NSA_LOCAL)
    work = jnp.where(valid & jnp.logical_not(forced), imp, -1.0)
    for _ in range(NSA_TOPN - NSA_LOCAL - 1):
        idx = jnp.argmax(work, axis=1, keepdims=True)
        work = jnp.where(j == idx, -2.0, work)
    sel = valid & (forced | (work == -2.0) | (cur < NSA_TOPN))
    bias = jnp.where(sel, 0.0, NEG_BIG).astype(BF16)
    for st in range(nbp // 64):
        placed = jnp.dot(bias, place_ref[st], preferred_element_type=F32).astype(BF16)
        for g in range(grp):
            qaug_ref[st, g * tq:(g + 1) * tq, :] = q4[g * tq:(g + 1) * tq] + placed

    wlen = NSA_WINDOW + tq
    w0 = pl.multiple_of(jnp.maximum(q0 - NSA_WINDOW, 0), LANES)
    s_w = _dot_nt(q4, kw_ref[0, pl.ds(w0, wlen), :])
    dpos = row_pos((rows, wlen)) - (w0 + lax.broadcasted_iota(jnp.int32, (rows, wlen), 1))
    s_w = jnp.where((dpos >= 0) & (dpos <= NSA_WINDOW), s_w, NEG_BIG)
    e_w = jnp.exp(s_w - jnp.max(s_w, axis=1, keepdims=True))
    l_w = jnp.sum(e_w, axis=1, keepdims=True)
    a_w = jnp.dot(e_w.astype(BF16), vs_ref[0, pl.ds(w0, wlen), :], preferred_element_type=F32) / l_w

    def tile(jt, carry, masked):
        k0 = pl.multiple_of(jt * tk, tk)
        qa = qaug_ref[jt // (NSA_SUPER // tk)]
        s = _dot_nt(qa, ka_ref[0, pl.ds(k0, tk), :])
        if masked:
            kpos = k0 + lax.broadcasted_iota(jnp.int32, (rows, tk), 1)
            s = jnp.where(kpos <= row_pos((rows, tk)), s, NEG_BIG)
        return _online_softmax_step(s, vs_ref[0, pl.ds(k0, tk), :], *carry)

    init = (jnp.full((rows, LANES), NEG_BIG, F32), jnp.zeros((rows, LANES), F32), jnp.zeros((rows, LANES), F32))
    jd = q0 // tk
    carry = lax.fori_loop(0, jd, lambda jt, c: tile(jt, c, False), init)
    _, l_s, acc_s = tile(jd, carry, True)
    a_s = acc_s / l_s

    gt = g_ref[0].reshape(rows, 3)
    lane = lax.broadcasted_iota(jnp.int32, (rows, LANES), 1)
    mix = jnp.where(lane < NSA_DH, gt[:, 1:2] * a_s + gt[:, 0:1] * o_c, gt[:, 2:3] * a_w)
    both = mix + pltpu.roll(mix, NSA_DH, axis=1)
    lane_q = lax.broadcasted_iota(jnp.int32, (tq, LANES), 1)
    for pr in range(grp // 2):
        even = both[(2 * pr) * tq:(2 * pr + 1) * tq]
        odd = both[(2 * pr + 1) * tq:(2 * pr + 2) * tq]
        o_ref[:, pr * LANES:(pr + 1) * LANES] = jnp.where(lane_q < NSA_DH, even, odd)


def nsa_prompt_attention(qp, gates, ka, kw, vs, ck, cv):
    hkv, grp, t, _ = qp.shape
    nbp = ck.shape[1]
    tq, tk = NSA_TQ, NSA_TK
    assert t % tk == 0 and tk % tq == 0 and t >= NSA_WINDOW + tq and nbp % LANES == 0 and nbp * NSA_BLK >= t
    nsup = nbp // 64
    blk = np.arange(nbp)
    place = np.zeros((nsup, nbp, LANES), np.float32)
    place[blk // 64, blk, NSA_DH + blk % 64] = 1.0
    return pl.pallas_call(
        functools.partial(_nsa_prompt_body, tq=tq, tk=tk, nbp=nbp),
        grid=(hkv, t // tq),
        in_specs=[
            pl.BlockSpec((1, grp, tq, LANES), lambda h, i: (h, 0, i, 0)),
            pl.BlockSpec((1, grp, tq, 3), lambda h, i: (h, 0, i, 0)),
            pl.BlockSpec((1, t, LANES), lambda h, i: (h, 0, 0)),
            pl.BlockSpec((1, t, LANES), lambda h, i: (h, 0, 0)),
            pl.BlockSpec((1, t, LANES), lambda h, i: (h, 0, 0)),
            pl.BlockSpec((1, nbp, LANES), lambda h, i: (h, 0, 0)),
            pl.BlockSpec((1, nbp, LANES), lambda h, i: (h, 0, 0)),
            pl.BlockSpec((nsup, nbp, LANES), lambda h, i: (0, 0, 0)),
        ],
        out_specs=pl.BlockSpec((tq, grp * NSA_DH), lambda h, i: (i, h)),
        out_shape=jax.ShapeDtypeStruct((t, hkv * grp * NSA_DH), F32),
        scratch_shapes=[pltpu.VMEM((nsup, grp * tq, LANES), BF16)],
        compiler_params=pltpu.CompilerParams(
            dimension_semantics=("parallel", "arbitrary"),
            vmem_limit_bytes=V7X_VMEM_LIMIT_BYTES),
        name="nsa_prompt",
    )(qp, gates, ka, kw, vs, ck, cv, jnp.asarray(place, BF16))


def nsa_prompt_pallas(q, kv, gates):
    n, _, _, t, _ = q.shape
    assert n == 1
    nb = t // NSA_BLK
    nbp = -(-nb // LANES) * LANES
    zpad = jnp.zeros((NSA_KV_HEADS, t, LANES - NSA_DH), F32)
    qp = jnp.concatenate([q[0] * NSA_SCALE, jnp.zeros(q.shape[1:4] + (LANES - NSA_DH,), F32)], axis=-1).astype(BF16)
    code = (jnp.arange(t)[:, None] // NSA_BLK) % 64 == jnp.arange(64)[None, :]
    ka = jnp.concatenate([kv[0, 1, 0], jnp.broadcast_to(code.astype(F32), (NSA_KV_HEADS, t, 64))], axis=-1).astype(BF16)
    kw = jnp.concatenate([kv[0, 2, 0], zpad], axis=-1).astype(BF16)
    vs = jnp.concatenate([kv[0, 1, 1], kv[0, 2, 1]], axis=-1).astype(BF16)
    means = kv[0, 0].reshape(2, NSA_KV_HEADS, nb, NSA_BLK, NSA_DH).mean(axis=3, dtype=F32)
    means = jnp.pad(means, ((0, 0), (0, 0), (0, nbp - nb), (0, LANES - NSA_DH))).astype(BF16)
    o = nsa_prompt_attention(qp, gates[0], ka, kw, vs, means[0], means[1])
    return o.reshape(n, t, NSA_HEADS * NSA_DH), kv[:, 2, :, :, t - min(NSA_WINDOW, t):]


MLA_DEC_PAGES = 16


def _mla_decode_body(pt_ref, qlat_ref, qpe_ref, cnew_ref, kpenew_ref, ksnew_ref, wuv_ref, *refs, npg):
    ckv_refs, kpe_refs, ks_refs = refs[0:npg], refs[npg:2 * npg], refs[2 * npg:3 * npg]
    o_ref, m_ref, l_ref, acc_ref = refs[3 * npg:]
    g = pl.program_id(1)

    @pl.when(g == 0)
    def _():
        m_ref[...] = jnp.full(m_ref.shape, NEG_BIG, F32)
        l_ref[...] = jnp.zeros(l_ref.shape, F32)
        acc_ref[...] = jnp.zeros(acc_ref.shape, F32)

    qlat = qlat_ref[0].astype(BF16)
    qpe = qpe_ref[0].astype(BF16)
    scs, cs = [], []
    for s in range(npg):
        c = ckv_refs[s][0].astype(BF16)
        content = _dot_nt(qlat, c)
        pe = jnp.dot(qpe, kpe_refs[s][0].astype(BF16), preferred_element_type=F32)
        scs.append((content * ks_refs[s][0] + pe) * MLA_SCALE)
        cs.append(c)
    sall = jnp.concatenate(scs, axis=1)
    m_old = m_ref[...]
    m_new = jnp.maximum(m_old, jnp.max(sall, axis=1, keepdims=True))
    alpha = jnp.exp(m_old - m_new)
    p32 = jnp.exp(sall - _lane_tile(m_new, npg))
    l_new = alpha * l_ref[...] + jnp.sum(p32, axis=1, keepdims=True)
    p = p32.astype(BF16)
    pv = jnp.dot(p[:, 0:PAGE_SIZE], cs[0], preferred_element_type=F32)
    for s in range(1, npg):
        pv = pv + jnp.dot(p[:, s * PAGE_SIZE:(s + 1) * PAGE_SIZE], cs[s], preferred_element_type=F32)
    acc_new = _lane_tile(alpha, MLA_KV_LORA // LANES) * acc_ref[...] + pv
    m_ref[...] = m_new
    l_ref[...] = l_new
    acc_ref[...] = acc_new

    @pl.when(g == pl.num_programs(1) - 1)
    def _():
        cn = _bf16_round(cnew_ref[0])
        content = jnp.sum(_bf16_round(qlat_ref[0]) * cn, axis=1, keepdims=True)
        pe = jnp.sum(_bf16_round(qpe_ref[0]) * _bf16_round(kpenew_ref[0]), axis=1, keepdims=True)
        sc = (content * _col_from_row(ksnew_ref[0]) + pe) * MLA_SCALE
        m_fin = jnp.maximum(m_new, sc)
        a = jnp.exp(m_new - m_fin)
        pn = jnp.exp(sc - m_fin)
        l_fin = a * l_new + pn
        acc = _lane_tile(a, MLA_KV_LORA // LANES) * acc_new + _bf16_round(pn[:, 0:1]) * cn
        o_lat = acc / _lane_tile(l_fin, MLA_KV_LORA // LANES)
        full = jnp.dot(o_lat.astype(BF16), wuv_ref[...], preferred_element_type=F32)
        r = lax.broadcasted_iota(jnp.int32, full.shape, 0)
        cidx = lax.broadcasted_iota(jnp.int32, full.shape, 1)
        o_ref[0] = jnp.sum(jnp.where(r == cidx // MLA_V, full, 0.0), axis=0, keepdims=True)


def mla_decode(q_lat, q_pe, c_new, kpe_new, ks_new, w_uv, cache_ckv, cache_kpe_t, cache_ks_t, page_table):
    n, nh, cdim = q_lat.shape
    rdim = q_pe.shape[2]
    npages = page_table.shape[1]
    npg = MLA_DEC_PAGES
    assert npages % npg == 0
    per_n3 = lambda i, g, pt: (i, 0, 0)

    def page_map(s):
        return lambda i, g, pt: (pt[i, g * npg + s], 0, 0)

    in_specs = [
        pl.BlockSpec((1, nh, cdim), per_n3),
        pl.BlockSpec((1, nh, rdim), per_n3),
        pl.BlockSpec((1, 1, cdim), per_n3),
        pl.BlockSpec((1, 1, rdim), per_n3),
        pl.BlockSpec((1, 1, nh), per_n3),
        pl.BlockSpec(w_uv.shape, lambda i, g, pt: (0, 0)),
    ]
    in_specs += [pl.BlockSpec((1, PAGE_SIZE, cdim), page_map(s)) for s in range(npg)]
    in_specs += [pl.BlockSpec((1, rdim, PAGE_SIZE), page_map(s)) for s in range(npg)]
    in_specs += [pl.BlockSpec((1, nh, PAGE_SIZE), page_map(s)) for s in range(npg)]
    grid_spec = pltpu.PrefetchScalarGridSpec(
        num_scalar_prefetch=1, grid=(n, npages // npg), in_specs=in_specs,
        out_specs=pl.BlockSpec((1, 1, w_uv.shape[1]), per_n3),
        scratch_shapes=[pltpu.VMEM((nh, LANES), F32), pltpu.VMEM((nh, LANES), F32), pltpu.VMEM((nh, cdim), F32)])
    return pl.pallas_call(
        functools.partial(_mla_decode_body, npg=npg),
        grid_spec=grid_spec,
        out_shape=jax.ShapeDtypeStruct((n, 1, w_uv.shape[1]), F32),
        compiler_params=pltpu.CompilerParams(
            dimension_semantics=("parallel", "arbitrary"),
            vmem_limit_bytes=V7X_VMEM_LIMIT_BYTES),
        name="mla_decode",
    )(page_table, q_lat, q_pe, c_new, kpe_new, ks_new, w_uv,
      *([cache_ckv] * npg), *([cache_kpe_t] * npg), *([cache_ks_t] * npg))


NSA_DEC_PAGES = 16
NSA_DEC_ROWS = 8
NSA_DEC_PICK = NSA_TOPN - NSA_LOCAL - 1
NSA_DEC_NSEL = NSA_TOPN - 1


def _nsa_dec_select_body(pt_ref, q_ref, place_ref, *refs, npg, ngrp):
    page_refs = refs[:npg]
    oc_ref, idx_ref, parts_ref = refs[npg:]
    g = pl.program_id(1)
    bpp = PAGE_SIZE // NSA_BLK
    lane = lax.broadcasted_iota(jnp.int32, (NSA_DH, LANES), 1)
    blks = [jnp.zeros((NSA_DH, LANES), F32) for _ in range(4)]
    for s in range(npg):
        for kv in range(2):
            for h in range(NSA_KV_HEADS):
                slab = page_refs[s][0, kv, h]
                for b in range(bpp):
                    inblk = (lane >= b * NSA_BLK) & (lane < (b + 1) * NSA_BLK)
                    mean = jnp.sum(jnp.where(inblk, slab, 0.0), axis=1, keepdims=True) * (1.0 / NSA_BLK)
                    t = kv * NSA_KV_HEADS + h
                    blks[t] = jnp.where(lane == s * bpp + b, mean, blks[t])
    for t in range(4):
        parts_ref[g, t] = blks[t]

    @pl.when(g == ngrp - 1)
    def _():
        nb = ngrp * npg * bpp
        j = lax.broadcasted_iota(jnp.int32, (1, nb), 1)
        slot = lax.broadcasted_iota(jnp.int32, (1, LANES), 1)
        row = lax.broadcasted_iota(jnp.int32, (NSA_DEC_ROWS, nb), 0)
        for h in range(NSA_KV_HEADS):
            ck_t = jnp.zeros((NSA_DH, nb), F32)
            cv_t = jnp.zeros((NSA_DH, nb), F32)
            for gg in range(ngrp):
                ck_t = ck_t + jnp.dot(parts_ref[gg, h].astype(BF16), place_ref[gg], preferred_element_type=F32)
                cv_t = cv_t + jnp.dot(parts_ref[gg, NSA_KV_HEADS + h].astype(BF16), place_ref[gg],
                                      preferred_element_type=F32)
            q = (q_ref[0, h] * NSA_SCALE).astype(BF16)
            s_c = jnp.dot(q, ck_t.astype(BF16), preferred_element_type=F32)
            e_c = jnp.exp(s_c - jnp.max(s_c, axis=1, keepdims=True))
            p_c = e_c / jnp.sum(e_c, axis=1, keepdims=True)
            oc_ref[0, h] = _dot_nt(p_c.astype(BF16), cv_t.astype(BF16))
            imp = jnp.sum(jnp.where(row < NSA_GROUP, p_c, 0.0), axis=0, keepdims=True)
            work = jnp.where((j >= 1) & (j <= nb - NSA_LOCAL), imp, -1.0)
            picks = jnp.where(slot == NSA_DEC_PICK + 1, nb - 1, 0)
            for r in range(NSA_DEC_PICK):
                idx = jnp.argmax(work, axis=1, keepdims=True)
                work = jnp.where(j == idx, -2.0, work)
                picks = jnp.where(slot == r, idx, picks)
            idx_ref[0, h:h + 1, :] = picks


def nsa_decode_select(q8, cache_cmp_t, page_table):
    n = q8.shape[0]
    npages = page_table.shape[1]
    npg = NSA_DEC_PAGES
    bpp = PAGE_SIZE // NSA_BLK
    assert npages % npg == 0 and npg * bpp <= LANES
    ngrp = npages // npg
    nb = npages * bpp
    assert nb - NSA_LOCAL >= NSA_DEC_PICK + 1
    place = np.zeros((ngrp, LANES, nb), np.float32)
    c = np.arange(npg * bpp)
    for gg in range(ngrp):
        place[gg, c, gg * npg * bpp + c] = 1.0

    def page_map(s):
        return lambda i, g, pt: (pt[i, g * npg + s], 0, 0, 0, 0)

    in_specs = [pl.BlockSpec((1, NSA_KV_HEADS, NSA_DEC_ROWS, NSA_DH), lambda i, g, pt: (i, 0, 0, 0)),
                pl.BlockSpec((ngrp, LANES, nb), lambda i, g, pt: (0, 0, 0))]
    in_specs += [pl.BlockSpec((1, 2, NSA_KV_HEADS, NSA_DH, PAGE_SIZE), page_map(s)) for s in range(npg)]
    grid_spec = pltpu.PrefetchScalarGridSpec(
        num_scalar_prefetch=1, grid=(n, ngrp), in_specs=in_specs,
        out_specs=[pl.BlockSpec((1, NSA_KV_HEADS, NSA_DEC_ROWS, NSA_DH), lambda i, g, pt: (i, 0, 0, 0)),
                   pl.BlockSpec((1, NSA_KV_HEADS, LANES), lambda i, g, pt: (i, 0, 0))],
        scratch_shapes=[pltpu.VMEM((ngrp, 2 * NSA_KV_HEADS, NSA_DH, LANES), F32)])
    return pl.pallas_call(
        functools.partial(_nsa_dec_select_body, npg=npg, ngrp=ngrp),
        grid_spec=grid_spec,
        out_shape=[jax.ShapeDtypeStruct((n, NSA_KV_HEADS, NSA_DEC_ROWS, NSA_DH), F32),
                   jax.ShapeDtypeStruct((n, NSA_KV_HEADS, LANES), jnp.int32)],
        compiler_params=pltpu.CompilerParams(
            dimension_semantics=("parallel", "arbitrary"),
            vmem_limit_bytes=V7X_VMEM_LIMIT_BYTES),
        name="nsa_decode_select",
    )(page_table, q8, jnp.asarray(place, BF16), *([cache_cmp_t] * npg))


def _nsa_dec_attend_body(pg_ref, half_ref, q_ref, gate_ref, oc_ref, new_ref, *refs, nsel):
    sel_refs = refs[:nsel]
    win_ref, o_ref, wout_ref = refs[nsel:]
    i, h = pl.program_id(0), pl.program_id(1)
    qf = _bf16_round(q_ref[0, 0] * NSA_SCALE)
    q = qf.astype(BF16)
    new = new_ref[0, 0]
    lane = lax.broadcasted_iota(jnp.int32, (NSA_DEC_ROWS, PAGE_SIZE), 1)

    def attend(score_tiles, value_tiles, k_new, v_new):
        s_new = jnp.sum(qf * _bf16_round(k_new), axis=1, keepdims=True)
        sall = jnp.concatenate(score_tiles, axis=1) if len(score_tiles) > 1 else score_tiles[0]
        m = jnp.maximum(jnp.max(sall, axis=1, keepdims=True), s_new)
        p = jnp.exp(sall - m)
        p_new = jnp.exp(s_new - m)
        l = jnp.sum(p, axis=1, keepdims=True) + p_new
        pb = p.astype(BF16)
        acc = _bf16_round(p_new) * _bf16_round(v_new)
        off = 0
        for vt in value_tiles:
            w = vt.shape[1]
            acc = acc + _dot_nt(pb[:, off:off + w], vt)
            off += w
        return acc / l

    s_tiles, v_tiles = [], []
    for b in range(nsel):
        hb = half_ref[(i * NSA_KV_HEADS + h) * nsel + b]
        sb = jnp.dot(q, sel_refs[b][0, 0, 0].astype(BF16), preferred_element_type=F32)
        s_tiles.append(jnp.where(lax.shift_right_logical(lane, 6) == hb, sb, NEG_BIG))
        v_tiles.append(sel_refs[b][0, 1, 0].astype(BF16))
    o_s = attend(s_tiles, v_tiles, new[0:1], new[1:2])
    s_w = jnp.dot(q, win_ref[0, 0, 0].astype(BF16), preferred_element_type=F32)
    o_w = attend([s_w], [win_ref[0, 1, 0].astype(BF16)], new[2:3], new[3:4])
    gt = gate_ref[0, 0]
    o_ref[0, 0] = gt[:, 0:1] * oc_ref[0, 0] + gt[:, 1:2] * o_s + gt[:, 2:3] * o_w

    wl = win_ref.shape[4]
    wlane = lax.broadcasted_iota(jnp.int32, (NSA_DH, wl), 1)
    for kv in range(2):
        col = _col_from_row(new[2 + kv:3 + kv])
        wout_ref[0, kv, 0] = jnp.where(wlane == wl - 1, col, pltpu.roll(win_ref[0, kv, 0], wl - 1, axis=1))


def nsa_decode_attend(pages, halves, q8, gates8, oc8, new_rows, cache_sel_t, win_t):
    n = q8.shape[0]
    nsel = NSA_DEC_NSEL
    wl = win_t.shape[4]
    per = lambda i, h, pg, hf: (i, h, 0, 0)

    def sel_map(b):
        return lambda i, h, pg, hf: (pg[(i * NSA_KV_HEADS + h) * nsel + b], 0, h, 0, 0)

    in_specs = [pl.BlockSpec((1, 1, NSA_DEC_ROWS, NSA_DH), per),
                pl.BlockSpec((1, 1, NSA_DEC_ROWS, 3), per),
                pl.BlockSpec((1, 1, NSA_DEC_ROWS, NSA_DH), per),
                pl.BlockSpec((1, 1, 4, NSA_DH), per)]
    in_specs += [pl.BlockSpec((1, 2, 1, NSA_DH, PAGE_SIZE), sel_map(b)) for b in range(nsel)]
    in_specs += [pl.BlockSpec((1, 2, 1, NSA_DH, wl), lambda i, h, pg, hf: (i, 0, h, 0, 0))]
    grid_spec = pltpu.PrefetchScalarGridSpec(
        num_scalar_prefetch=2, grid=(n, NSA_KV_HEADS), in_specs=in_specs,
        out_specs=[pl.BlockSpec((1, 1, NSA_DEC_ROWS, NSA_DH), per),
                   pl.BlockSpec((1, 2, 1, NSA_DH, wl), lambda i, h, pg, hf: (i, 0, h, 0, 0))])
    return pl.pallas_call(
        functools.partial(_nsa_dec_attend_body, nsel=nsel),
        grid_spec=grid_spec,
        out_shape=[jax.ShapeDtypeStruct((n, NSA_KV_HEADS, NSA_DEC_ROWS, NSA_DH), F32),
                   jax.ShapeDtypeStruct(win_t.shape, F32)],
        compiler_params=pltpu.CompilerParams(
            dimension_semantics=("parallel", "arbitrary"),
            vmem_limit_bytes=V7X_VMEM_LIMIT_BYTES),
        name="nsa_decode_attend",
    )(pages, halves, q8, gates8, oc8, new_rows, *([cache_sel_t] * nsel), win_t)


def nsa_sample_pallas(q, kv, gates, cache_cmp, cache_sel, win_state, page_table):
    n = q.shape[0]
    assert q.shape[3] == 1 and win_state.shape[3] == NSA_WINDOW
    bpp = PAGE_SIZE // NSA_BLK
    padrows = ((0, 0), (0, 0), (0, NSA_DEC_ROWS - NSA_GROUP), (0, 0))
    q8 = jnp.pad(q[:, :, :, 0], padrows)
    gates8 = jnp.pad(gates[:, :, :, 0], padrows)
    feature_major = (0, 1, 2, 4, 3)
    oc8, picks = nsa_decode_select(q8, jnp.transpose(cache_cmp, feature_major), page_table)
    idx = picks[:, :, :NSA_DEC_NSEL]
    pages = jnp.take_along_axis(page_table[:, None, :], idx // bpp, axis=2)
    new_rows = jnp.stack([kv[:, 1, 0, :, 0], kv[:, 1, 1, :, 0], kv[:, 2, 0, :, 0], kv[:, 2, 1, :, 0]], axis=2)
    o8, win_t = nsa_decode_attend(pages.reshape(-1), (idx % bpp).reshape(-1), q8, gates8, oc8, new_rows,
                                  jnp.transpose(cache_sel, feature_major), jnp.transpose(win_state, feature_major))
    o = o8[:, :, :NSA_GROUP].reshape(n, 1, NSA_HEADS * NSA_DH)
    return o, jnp.transpose(win_t, feature_major)


def split_cols(x, sizes):
    return jnp.split(x, np.cumsum(sizes)[:-1].tolist(), axis=-1)


def rms_norm(x, g):
    xf = x.astype(F32)
    y = xf * lax.rsqrt(jnp.mean(xf * xf, axis=-1, keepdims=True) + EPS)
    return (y * g.astype(F32)).astype(x.dtype)


def layer_norm(x, g, b):
    xf = x.astype(F32)
    xc = xf - jnp.mean(xf, axis=-1, keepdims=True)
    y = xc * lax.rsqrt(jnp.mean(xc * xc, axis=-1, keepdims=True) + EPS)
    return (y * g.astype(F32) + b.astype(F32)).astype(x.dtype)


def l2_normalize(x):
    xf = x.astype(F32)
    return xf * lax.rsqrt(jnp.sum(xf * xf, axis=-1, keepdims=True) + EPS)


def rope(x, pos):
    d = x.shape[-1]
    inv = ROPE_THETA ** (-jnp.arange(0, d, 2, dtype=F32) / d)
    ang = pos.astype(F32)[..., None] * inv
    cos, sin = jnp.cos(ang), jnp.sin(ang)
    xf = x.astype(F32)
    x1, x2 = xf[..., : d // 2], xf[..., d // 2:]
    return jnp.concatenate([x1 * cos - x2 * sin, x2 * cos + x1 * sin], axis=-1).astype(x.dtype)


def partial_rope(x, pos, rot):
    return jnp.concatenate([rope(x[..., :rot], pos), x[..., rot:]], axis=-1)


def causal_conv(x_ext, w):
    t = x_ext.shape[1] - (GDN_CONV - 1)
    return sum(x_ext[:, i:i + t] * w[i] for i in range(GDN_CONV))


def gdn_prepare(conv_out, b_raw, a_raw, a_log, dt_bias):
    n, t, _ = conv_out.shape
    q, k, v = split_cols(conv_out, (GDN_HEADS * GDN_DK, GDN_HEADS * GDN_DK, GDN_HEADS * GDN_DV))
    q = l2_normalize(q.reshape(n, t, GDN_HEADS, GDN_DK)) * (GDN_DK ** -0.5)
    k = l2_normalize(k.reshape(n, t, GDN_HEADS, GDN_DK))
    v = v.reshape(n, t, GDN_HEADS, GDN_DV).astype(F32)
    beta = jax.nn.sigmoid(b_raw.astype(F32))
    g = -jnp.exp(a_log.astype(F32)) * jax.nn.softplus(a_raw.astype(F32) + dt_bias.astype(F32))
    return q, k, v, beta, g


GDN_SUPER = 256


def _bmm(a, b):
    return jnp.einsum('hij,hjk->hik', a, b, preferred_element_type=F32)


def _gdn_chunk_body(q_ref, k_ref, kt_ref, v_ref, b_ref, g_ref, o_ref, sfin_ref, s_ref, vnew_ref):
    n, c, nh = GDN_SUPER, GDN_CHUNK, GDN_HEADS

    @pl.when(pl.program_id(0) == 0)
    def _():
        s_ref[...] = jnp.zeros(s_ref.shape, F32)
        vnew_ref[...] = jnp.zeros(vnew_ref.shape, BF16)

    r = lax.broadcasted_iota(jnp.int32, (n, n), 0)
    cc = lax.broadcasted_iota(jnp.int32, (n, n), 1)
    same = lax.shift_right_logical(r, 6) == lax.shift_right_logical(cc, 6)
    tril = same & (cc <= r)
    stril = same & (cc < r)
    last = same & ((cc & (c - 1)) == c - 1)
    eye = (cc == r).astype(F32)
    gc_all = jnp.dot(tril.astype(F32), g_ref[...], precision=lax.Precision.HIGHEST, preferred_element_type=F32)
    b_all = b_ref[...]
    q, k, v = q_ref[...], k_ref[...], v_ref[...]
    ms, decays, vbs, kbgs, qgs, kdts, gls = [], [], [], [], [], [], []
    for h in range(nh):
        gc = gc_all[:, h:h + 1]
        beta = b_all[:, h:h + 1]
        gc_row = jnp.sum(jnp.where(cc == r, jnp.broadcast_to(gc, (n, n)), 0.0), axis=0, keepdims=True)
        glast_col = jnp.sum(jnp.where(last, jnp.broadcast_to(gc_row, (n, n)), 0.0), axis=1, keepdims=True)
        glast_row = jnp.sum(jnp.where(cc == r, jnp.broadcast_to(glast_col, (n, n)), 0.0), axis=0, keepdims=True)
        decays.append(jnp.where(tril, jnp.exp(jnp.where(tril, gc - gc_row, 0.0)), 0.0))
        eg = jnp.exp(gc)
        kb = k[h] * beta
        ms.append(kb.astype(BF16))
        vbs.append((v[h] * beta).astype(BF16))
        kbgs.append((kb * eg).astype(BF16))
        qgs.append((q[h] * eg).astype(BF16))
        kdts.append((kt_ref[h] * jnp.exp(glast_row - gc_row)).astype(BF16))
        gls.append(jnp.exp(glast_col))
    decay = jnp.stack(decays)
    kbf = k.astype(BF16)
    kkt = jnp.einsum('hid,hjd->hij', jnp.stack(ms), kbf, preferred_element_type=F32)
    m = jnp.where(stril[None], kkt * decay, 0.0)
    attn = jnp.where(tril[None], jnp.einsum('hid,hjd->hij', q.astype(BF16), kbf, preferred_element_type=F32) * decay,
                     0.0).astype(BF16)
    tinv = eye[None] - m
    pw = m
    for _ in range(int(np.log2(c)) - 1):
        pwb = pw.astype(BF16)
        pw = _bmm(pwb, pwb)
        tinv = tinv + _bmm(tinv.astype(BF16), pw.astype(BF16))
    tb = tinv.astype(BF16)
    u = _bmm(tb, jnp.stack(vbs))
    w = _bmm(tb, jnp.stack(kbgs)).astype(BF16)
    qg = jnp.stack(qgs)
    kdt = jnp.stack(kdts)
    gl = jnp.stack(gls)
    state = s_ref[...]
    rown = lax.broadcasted_iota(jnp.int32, (n, GDN_DV), 0)
    for ci in range(n // c):
        rows = slice(ci * c, (ci + 1) * c)
        sb = state.astype(BF16)
        vb = (u[:, rows] - _bmm(w[:, rows], sb)).astype(BF16)
        vnew_ref[:, rows, :] = vb
        vall = vnew_ref[...]
        o_ref[:, rows, :] = _bmm(qg[:, rows], sb) + _bmm(attn[:, rows], vall)
        vonly = jnp.where((lax.shift_right_logical(rown, 6) == ci)[None], vall, jnp.zeros_like(vall))
        state = state * gl[:, ci * c:ci * c + 1] + _bmm(kdt, vonly)
    s_ref[...] = state
    sfin_ref[...] = state


def gdn_chunked_pallas(q, k, kt, v, beta, g):
    nh, t, _ = q.shape
    n = GDN_SUPER
    assert t % n == 0 and GDN_DK == GDN_DV and n % GDN_CHUNK == 0 and GDN_CHUNK == 64
    hrow = lambda i: (0, i, 0)
    return pl.pallas_call(
        _gdn_chunk_body,
        grid=(t // n,),
        in_specs=[pl.BlockSpec((nh, n, GDN_DK), hrow), pl.BlockSpec((nh, n, GDN_DK), hrow),
                  pl.BlockSpec((nh, GDN_DK, n), lambda i: (0, 0, i)), pl.BlockSpec((nh, n, GDN_DV), hrow),
                  pl.BlockSpec((n, nh), lambda i: (i, 0)), pl.BlockSpec((n, nh), lambda i: (i, 0))],
        out_specs=[pl.BlockSpec((nh, n, GDN_DV), hrow), pl.BlockSpec((nh, GDN_DK, GDN_DV), lambda i: (0, 0, 0))],
        out_shape=[jax.ShapeDtypeStruct((nh, t, GDN_DV), F32), jax.ShapeDtypeStruct((nh, GDN_DK, GDN_DV), F32)],
        scratch_shapes=[pltpu.VMEM((nh, GDN_DK, GDN_DV), F32), pltpu.VMEM((nh, n, GDN_DV), BF16)],
        compiler_params=pltpu.CompilerParams(
            dimension_semantics=("arbitrary",),
            vmem_limit_bytes=V7X_VMEM_LIMIT_BYTES),
        name="gdn_chunked",
    )(q, k, kt, v, beta, g)


def gdn_recurrent(q, k, v, beta, g, s0):
    def step(s, xs):
        q_t, k_t, v_t, b_t, g_t = xs
        s = s * jnp.exp(g_t)[..., None, None]
        delta = (v_t - jnp.einsum('nhde,nhd->nhe', s, k_t)) * b_t[..., None]
        s = s + jnp.einsum('nhd,nhe->nhde', k_t, delta)
        return s, jnp.einsum('nhde,nhd->nhe', s, q_t)

    xs = tuple(jnp.moveaxis(x, 1, 0) for x in (q, k, v, beta, g))
    s_fin, o = lax.scan(step, s0, xs)
    return jnp.moveaxis(o, 0, 1), s_fin


def gdn_output(o, z, out_norm):
    n, t = z.shape[:2]
    gate = jax.nn.silu(z.reshape(n, t, GDN_HEADS, GDN_DV).astype(F32))
    return (rms_norm(o, out_norm) * gate).astype(z.dtype).reshape(n, t, GDN_HEADS * GDN_DV)


def even_project(xn, pos, w_in, q_norm, k_norm):
    n, t, _ = xn.shape
    qkv, z, b_raw, a_raw, q, kv, g_raw = split_cols(xn @ w_in, EVEN_SIZES)
    q = partial_rope(rms_norm(q.reshape(n, t, NSA_HEADS, NSA_DH), q_norm), pos[:, None], NSA_ROT)
    q = jnp.transpose(q.reshape(n, t, NSA_KV_HEADS, NSA_GROUP, NSA_DH), (0, 2, 3, 1, 4))
    kv = kv.reshape(n, t, 3, 2, NSA_KV_HEADS, NSA_DH)
    k = partial_rope(rms_norm(kv[:, :, :, 0], k_norm[:, None, :]), pos[:, None, None], NSA_ROT)
    kv = jnp.transpose(jnp.stack([k, kv[:, :, :, 1]], axis=3), (0, 2, 3, 4, 1, 5))
    gates = jax.nn.sigmoid(g_raw.astype(F32)).reshape(n, t, NSA_KV_HEADS, NSA_GROUP, 3)
    return qkv, z, b_raw, a_raw, q, kv, jnp.transpose(gates, (0, 2, 3, 1, 4))


def even_prompt(xn, pos, w_in, w_out, conv_w, a_log, dt_bias, out_norm, q_norm, k_norm):
    n = xn.shape[0]
    qkv, z, b_raw, a_raw, q, kv, gates = even_project(xn, pos, w_in, q_norm, k_norm)
    qkv_ext = jnp.concatenate([jnp.zeros((n, GDN_CONV - 1, GDN_CONV_CH), qkv.dtype), qkv], axis=1)
    gq, gk, gv, beta, g = gdn_prepare(jax.nn.silu(causal_conv(qkv_ext, conv_w)), b_raw, a_raw, a_log, dt_bias)
    assert n == 1
    qh, kh, vh = (jnp.transpose(x[0], (1, 0, 2)) for x in (gq, gk, gv))
    o_h, s_fin = gdn_chunked_pallas(qh, kh, jnp.transpose(kh, (0, 2, 1)), vh, beta[0], g[0])
    o, s_fin = jnp.transpose(o_h, (1, 0, 2))[None], s_fin[None]
    nsa_out, win = nsa_prompt_pallas(q, kv, gates)
    y = jnp.concatenate([gdn_output(o, z, out_norm), nsa_out], axis=-1) @ w_out
    return y, (s_fin.astype(xn.dtype), qkv_ext[:, -(GDN_CONV - 1):], kv[:, 0], kv[:, 1], win)


def even_sample(xn, pos, gdn_state, conv_state, cache_cmp, cache_sel, win_state, page_table,
                w_in, w_out, conv_w, a_log, dt_bias, out_norm, q_norm, k_norm):
    qkv, z, b_raw, a_raw, q, kv, gates = even_project(xn, pos, w_in, q_norm, k_norm)
    qkv_ext = jnp.concatenate([conv_state.astype(qkv.dtype), qkv], axis=1)
    gq, gk, gv, beta, g = gdn_prepare(jax.nn.silu(causal_conv(qkv_ext, conv_w)), b_raw, a_raw, a_log, dt_bias)
    o, s_fin = gdn_recurrent(gq, gk, gv, beta, g, gdn_state.astype(F32))
    nsa_out, win = nsa_sample_pallas(q, kv, gates, cache_cmp, cache_sel, win_state, page_table)
    y = jnp.concatenate([gdn_output(o, z, out_norm), nsa_out], axis=-1) @ w_out
    return y, (s_fin.astype(xn.dtype), qkv_ext[:, -(GDN_CONV - 1):], kv[:, 0], kv[:, 1], win)


def sgu_mix(u, v, w_s, b_s):
    n, t, _ = u.shape
    l = min(t, SGU_CHUNK)
    nc = t // l
    w = jnp.tril(w_s[:, :l, :l])
    mix = jnp.einsum('gts,ncsgd->nctgd', w, v.reshape(n, nc, l, SGU_GROUPS, SGU_DG))
    mix = mix + jnp.transpose(b_s[:, :l])[:, :, None]
    return (u.reshape(n, nc, l, SGU_GROUPS, SGU_DG) * mix).reshape(n, t, SGU_WIDTH)


def odd_project(xn, pos, w_in, ln_g, ln_b, cq_norm, ckv_norm, w_uq, w_uk, qn_norm, qr_norm, kr_norm):
    n, t, _ = xn.shape
    uv, cq, ckv, kr = split_cols(xn @ w_in, ODD_SIZES)
    uv = jax.nn.gelu(uv, approximate=False)
    u = uv[..., :SGU_WIDTH]
    v = layer_norm(uv[..., SGU_WIDTH:], ln_g, ln_b)
    q = (rms_norm(cq, cq_norm) @ w_uq).reshape(n, t, MLA_HEADS, MLA_NOPE + MLA_ROPE)
    q_nope = rms_norm(q[..., :MLA_NOPE], qn_norm)
    q_pe = rope(rms_norm(q[..., MLA_NOPE:], qr_norm), pos[:, None])
    c = rms_norm(ckv, ckv_norm)
    k_pe = rope(rms_norm(kr, kr_norm), pos)
    k_nope = jnp.einsum('ntc,chd->nthd', c, w_uk).astype(F32)
    kscale = lax.rsqrt(jnp.mean(k_nope * k_nope, axis=-1) + EPS)
    return u, v, q_nope, q_pe, c, k_pe, k_nope, kscale


def odd_prompt(xn, pos, w_in, w_out, ln_g, ln_b, sgu_w, sgu_b, cq_norm, ckv_norm, w_uq, w_uk, w_uv,
               qn_norm, qr_norm, kn_norm, kr_norm):
    n, t, _ = xn.shape
    u, v, q_nope, q_pe, c, k_pe, k_nope, kscale = odd_project(
        xn, pos, w_in, ln_g, ln_b, cq_norm, ckv_norm, w_uq, w_uk, qn_norm, qr_norm, kr_norm)
    sgu = sgu_mix(u, v, sgu_w, sgu_b)
    k_nope_n = (k_nope * kscale[..., None] * kn_norm.astype(F32)).astype(xn.dtype)
    pad = jnp.zeros((n, t, MLA_HEADS, LANES - MLA_NOPE - MLA_ROPE), F32)
    q = jnp.concatenate([q_nope * MLA_SCALE, q_pe * MLA_SCALE, pad], axis=-1)
    k = jnp.concatenate([k_nope_n, jnp.broadcast_to(k_pe[:, :, None, :], (n, t, MLA_HEADS, MLA_ROPE)), pad], axis=-1)
    vv = jnp.einsum('ntc,chd->nthd', c, w_uv)
    assert n == 1
    att = flash_causal_pairs(q.reshape(t, MLA_HEADS * LANES).astype(BF16),
                             k.reshape(t, MLA_HEADS * LANES).astype(BF16),
                             vv.reshape(t, MLA_HEADS * MLA_V).astype(BF16)).reshape(n, t, MLA_HEADS * MLA_V)
    y = jnp.concatenate([sgu, att], axis=-1) @ w_out
    return y, (c, k_pe, kscale.astype(xn.dtype))


def odd_sample(xn, pos, cache_ckv, cache_kpe, cache_kscale, page_table, w_in, w_out, ln_g, ln_b, sgu_w,
               sgu_b, cq_norm, ckv_norm, w_uq, w_uk, w_uv, qn_norm, qr_norm, kn_norm, kr_norm):
    n, s, _ = xn.shape
    u, v, q_nope, q_pe, c, k_pe, k_nope, kscale = odd_project(
        xn, pos, w_in, ln_g, ln_b, cq_norm, ckv_norm, w_uq, w_uk, qn_norm, qr_norm, kr_norm)
    sgu = sgu_mix(u, v, sgu_w, sgu_b)
    q_lat = jnp.einsum('nqhd,chd->nqhc', q_nope.astype(F32) * kn_norm.astype(F32), w_uk)
    assert s == 1
    att = mla_decode(q_lat[:, 0], q_pe[:, 0], c, k_pe, kscale.astype(F32),
                     w_uv.reshape(MLA_KV_LORA, MLA_HEADS * MLA_V).astype(BF16), cache_ckv,
                     jnp.swapaxes(cache_kpe, 1, 2), jnp.swapaxes(cache_kscale, 1, 2), page_table)
    y = jnp.concatenate([sgu, att], axis=-1) @ w_out
    return y, (c, k_pe, kscale.astype(xn.dtype), v)


def kernel(x_prompt, x_sample, state_gdn, state_gdn_conv, cache_nsa_cmp, cache_nsa_sel, state_nsa_win,
           cache_mla_ckv, cache_mla_kpe, cache_mla_kscale, page_table, norm_mix, norm_ffn, ffn_up, ffn_down,
           w_in_even, w_out_even, gdn_conv_w, gdn_a_log, gdn_dt_bias, gdn_out_norm, nsa_q_norm, nsa_k_norm,
           w_in_odd, w_out_odd, sgu_ln_g, sgu_ln_b, sgu_w, sgu_b, mla_cq_norm, mla_ckv_norm, mla_w_uq,
           mla_w_uk, mla_w_uv, mla_qn_norm, mla_qr_norm, mla_kn_norm, mla_kr_norm):
    pos_p = jnp.arange(x_prompt.shape[1], dtype=jnp.int32)
    pos_s = page_table.shape[1] * PAGE_SIZE + jnp.arange(x_sample.shape[1], dtype=jnp.int32)
    ffn_up_b = ffn_up.astype(BF16)
    ffn_down_b = ffn_down.astype(BF16)
    hp, hs = x_prompt, x_sample
    for layer in range(norm_mix.shape[0]):
        xp = rms_norm(hp, norm_mix[layer])
        xs = rms_norm(hs, norm_mix[layer])
        if layer % 2 == 0:
            mp, (p_gdn_state, p_gdn_conv, p_nsa_cmp, p_nsa_sel, p_nsa_win) = even_prompt(
                xp, pos_p, w_in_even, w_out_even, gdn_conv_w, gdn_a_log, gdn_dt_bias, gdn_out_norm,
                nsa_q_norm, nsa_k_norm)
            ms, (s_gdn_state, s_gdn_conv, s_nsa_cmp, s_nsa_sel, s_nsa_win) = even_sample(
                xs, pos_s, state_gdn, state_gdn_conv, cache_nsa_cmp, cache_nsa_sel, state_nsa_win, page_table,
                w_in_even, w_out_even, gdn_conv_w, gdn_a_log, gdn_dt_bias, gdn_out_norm, nsa_q_norm, nsa_k_norm)
        else:
            mp, (p_mla_ckv, p_mla_kpe, p_mla_kscale) = odd_prompt(
                xp, pos_p, w_in_odd, w_out_odd, sgu_ln_g, sgu_ln_b, sgu_w, sgu_b, mla_cq_norm, mla_ckv_norm,
                mla_w_uq, mla_w_uk, mla_w_uv, mla_qn_norm, mla_qr_norm, mla_kn_norm, mla_kr_norm)
            ms, (s_mla_ckv, s_mla_kpe, s_mla_kscale, s_sgu_v) = odd_sample(
                xs, pos_s, cache_mla_ckv, cache_mla_kpe, cache_mla_kscale, page_table, w_in_odd, w_out_odd,
                sgu_ln_g, sgu_ln_b, sgu_w, sgu_b, mla_cq_norm, mla_ckv_norm, mla_w_uq, mla_w_uk, mla_w_uv,
                mla_qn_norm, mla_qr_norm, mla_kn_norm, mla_kr_norm)
        hp = sq_relu_mlp(hp + mp, norm_ffn[layer], ffn_up_b[layer], ffn_down_b[layer])
        hs = sq_relu_mlp(hs + ms, norm_ffn[layer], ffn_up_b[layer], ffn_down_b[layer])
    return (hp, hs, p_gdn_state, p_gdn_conv, p_nsa_cmp, p_nsa_sel, p_nsa_win, p_mla_ckv, p_mla_kpe, p_mla_kscale,
            s_gdn_state, s_gdn_conv, s_nsa_cmp, s_nsa_sel, s_nsa_win, s_mla_ckv, s_mla_kpe, s_mla_kscale, s_sgu_v)
```

```python
import functools

import jax
import jax.numpy as jnp
import numpy as np
from jax import lax
from jax.experimental import pallas as pl
from jax.experimental.pallas import tpu as pltpu

F32 = jnp.float32
BF16 = jnp.bfloat16

D_MODEL = 1024
PAGE_SIZE = 128
ROPE_THETA = 500000.0
EPS = 1e-6

GDN_HEADS = 4
GDN_DK = 128
GDN_DV = 128
GDN_CONV = 4
GDN_CHUNK = 64
GDN_CONV_CH = GDN_HEADS * (2 * GDN_DK + GDN_DV)

NSA_HEADS = 8
NSA_KV_HEADS = 2
NSA_GROUP = NSA_HEADS // NSA_KV_HEADS
NSA_DH = 64
NSA_ROT = NSA_DH // 4
NSA_BLK = 64
NSA_TOPN = 16
NSA_LOCAL = 2
NSA_WINDOW = 512
NSA_SCALE = NSA_DH ** -0.5

SGU_GROUPS = 4
SGU_DG = 128
SGU_CHUNK = 128
SGU_WIDTH = SGU_GROUPS * SGU_DG

MLA_HEADS = 8
MLA_Q_LORA = 256
MLA_KV_LORA = 256
MLA_NOPE = 64
MLA_ROPE = 32
MLA_V = 64
MLA_SCALE = (MLA_NOPE + MLA_ROPE) ** -0.5

EVEN_SIZES = (GDN_CONV_CH, GDN_HEADS * GDN_DV, GDN_HEADS, GDN_HEADS,
              NSA_HEADS * NSA_DH, 3 * 2 * NSA_KV_HEADS * NSA_DH, 3 * NSA_HEADS)
ODD_SIZES = (2 * SGU_WIDTH, MLA_Q_LORA, MLA_KV_LORA, MLA_ROPE)

V7X_VMEM_LIMIT_BYTES = 56 * 1024 * 1024
LANES = 128
NEG_BIG = -1e30


def _dot_nt(a, b):
    return lax.dot_general(a, b, (((1,), (1,)), ((), ())), preferred_element_type=F32)


def _bf16_round(x):
    return x.astype(BF16).astype(F32)


def _lane_tile(x, k):
    return x if k == 1 else jnp.concatenate([x] * k, axis=1)


def _col_from_row(row):
    k = row.shape[1]
    r = lax.broadcasted_iota(jnp.int32, (k, k), 0)
    c = lax.broadcasted_iota(jnp.int32, (k, k), 1)
    return jnp.sum(jnp.where(r == c, jnp.broadcast_to(row, (k, k)), 0.0), axis=1, keepdims=True)


def _mlp_body(x_ref, g_ref, wu_ref, wd_ref, o_ref, xn_ref, acc_ref):
    f = pl.program_id(1)

    @pl.when(f == 0)
    def _():
        x = x_ref[...]
        y = x * lax.rsqrt(jnp.mean(x * x, axis=-1, keepdims=True) + EPS)
        xn_ref[...] = (y * g_ref[...]).astype(BF16)
        acc_ref[...] = jnp.zeros_like(acc_ref)

    z = jnp.dot(xn_ref[...], wu_ref[...], preferred_element_type=F32)
    a = jnp.square(jnp.maximum(z, 0.0)).astype(BF16)
    acc_ref[...] += jnp.dot(a, wd_ref[...], preferred_element_type=F32)

    @pl.when(f == pl.num_programs(1) - 1)
    def _():
        o_ref[...] = x_ref[...] + acc_ref[...]


def mlp_tiles(t):
    tm = min(t, 1024)
    tf = 512
    return tm, tf


def sq_relu_mlp(h, g, w_up, w_down):
    n, t, d = h.shape
    rows = n * t
    x = h.reshape(rows, d)
    tm, tf = mlp_tiles(rows)
    dff = w_up.shape[1]
    out = pl.pallas_call(
        _mlp_body,
        grid=(rows // tm, dff // tf),
        in_specs=[
            pl.BlockSpec((tm, d), lambda i, f: (i, 0)),
            pl.BlockSpec((1, d), lambda i, f: (0, 0)),
            pl.BlockSpec((d, tf), lambda i, f: (0, f)),
            pl.BlockSpec((tf, d), lambda i, f: (f, 0)),
        ],
        out_specs=pl.BlockSpec((tm, d), lambda i, f: (i, 0)),
        out_shape=jax.ShapeDtypeStruct((rows, d), F32),
        scratch_shapes=[pltpu.VMEM((tm, d), BF16), pltpu.VMEM((tm, d), F32)],
        compiler_params=pltpu.CompilerParams(
            dimension_semantics=("parallel", "arbitrary"),
            vmem_limit_bytes=V7X_VMEM_LIMIT_BYTES),
        name="sq_relu_mlp",
    )(x, g.reshape(1, d), w_up, w_down)
    return out.reshape(n, t, d)


FLASH_TQ = 1024
FLASH_TK = 1024


def _online_softmax_step(s, v, m, l, acc):
    tk = s.shape[1]
    m_new = jnp.maximum(m, jnp.max(s, axis=1, keepdims=True))
    alpha = jnp.exp(m - m_new)
    p = jnp.exp(s - _lane_tile(m_new, tk // LANES))
    l_new = alpha * l + jnp.sum(p, axis=1, keepdims=True)
    pv = jnp.dot(p.astype(BF16), v, preferred_element_type=F32)
    acc_new = _lane_tile(alpha, acc.shape[1] // LANES) * acc + pv
    return m_new, l_new, acc_new


def _flash_pair_body(q_ref, k_ref, v_ref, o_ref, *, tq, tk, dv):
    i = pl.program_id(1)
    q = [q_ref[:, h * LANES:(h + 1) * LANES] for h in range(2)]

    def tile(j, carry, masked):
        k0 = pl.multiple_of(j * tk, tk)
        v = v_ref[pl.ds(k0, tk), :]
        out = []
        for h in range(2):
            k = k_ref[pl.ds(k0, tk), h * LANES:(h + 1) * LANES]
            s = _dot_nt(q[h], k)
            if masked:
                rows = i * tq + lax.broadcasted_iota(jnp.int32, (tq, tk), 0)
                cols = k0 + lax.broadcasted_iota(jnp.int32, (tq, tk), 1)
                s = jnp.where(cols <= rows, s, NEG_BIG)
            out.append(_online_softmax_step(s, v, *carry[h]))
        return tuple(out)

    init = tuple((jnp.full((tq, LANES), NEG_BIG, F32), jnp.zeros((tq, LANES), F32),
                  jnp.zeros((tq, LANES), F32)) for _ in range(2))
    nfull = i * (tq // tk)
    carry = lax.fori_loop(0, nfull, lambda j, c: tile(j, c, False), init)
    for d in range(tq // tk):
        carry = tile(nfull + d, carry, True)
    (_, l0, a0), (_, l1, a1) = carry
    lane = lax.broadcasted_iota(jnp.int32, (tq, LANES), 1)
    o_ref[...] = jnp.where(lane < dv, a0 / l0, a1 / l1)


def flash_causal_pairs(q, k, v):
    t = q.shape[0]
    nh = q.shape[1] // LANES
    dv = v.shape[1] // nh
    assert 2 * dv == LANES and nh % 2 == 0 and t % FLASH_TQ == 0 and FLASH_TQ % FLASH_TK == 0
    return pl.pallas_call(
        functools.partial(_flash_pair_body, tq=FLASH_TQ, tk=FLASH_TK, dv=dv),
        grid=(nh // 2, t // FLASH_TQ),
        in_specs=[
            pl.BlockSpec((FLASH_TQ, 2 * LANES), lambda hp, i: (i, hp)),
            pl.BlockSpec((t, 2 * LANES), lambda hp, i: (0, hp)),
            pl.BlockSpec((t, LANES), lambda hp, i: (0, hp)),
        ],
        out_specs=pl.BlockSpec((FLASH_TQ, LANES), lambda hp, i: (i, hp)),
        out_shape=jax.ShapeDtypeStruct((t, nh * dv), F32),
        compiler_params=pltpu.CompilerParams(
            dimension_semantics=("parallel", "arbitrary"),
            vmem_limit_bytes=V7X_VMEM_LIMIT_BYTES),
        name="mla_flash",
    )(q, k, v)


NSA_TQ = 256
NSA_TK = 1024
NSA_SUPER = 64 * NSA_BLK


def _nsa_prompt_body(q_ref, g_ref, ka_ref, kw_ref, vs_ref, ck_ref, cv_ref, place_ref, o_ref, qaug_ref,
                     *, tq, tk, nbp):
    grp = NSA_GROUP
    rows = grp * tq
    i = pl.program_id(1)
    q0 = i * tq
    q4 = q_ref[0].reshape(rows, LANES)

    def row_pos(shape):
        r = lax.broadcasted_iota(jnp.int32, shape, 0)
        return q0 + (r & (tq - 1))

    s_c = _dot_nt(q4, ck_ref[0])
    jc = lax.broadcasted_iota(jnp.int32, (rows, nbp), 1)
    cmask = (jc + 1) * NSA_BLK <= row_pos((rows, nbp)) + 1
    s_c = jnp.where(cmask, s_c, NEG_BIG)
    e_c = jnp.where(cmask, jnp.exp(s_c - jnp.max(s_c, axis=1, keepdims=True)), 0.0)
    l_c = jnp.sum(e_c, axis=1, keepdims=True)
    p_c = e_c / jnp.maximum(l_c, 1e-30)
    o_c = jnp.dot(p_c.astype(BF16), cv_ref[0], preferred_element_type=F32)
    imp = p_c[0:tq]
    for g in range(1, grp):
        imp = imp + p_c[g * tq:(g + 1) * tq]

    j = lax.broadcasted_iota(jnp.int32, (tq, nbp), 1)
    cur = lax.shift_right_logical(q0 + lax.broadcasted_iota(jnp.int32, (tq, nbp), 0), 6)
    valid = j <= cur
    forced = (j == 0) | (j > cur - NSA_LOCAL)
    work = jnp.where(valid & jnp.logical_not(forced), imp, -1.0)
    jf = j.astype(F32)
    for _ in range(NSA_TOPN - NSA_LOCAL - 1):
        top = jnp.max(work, axis=1, keepdims=True)
        first = jnp.min(jnp.where(work == top, jf, float(nbp)), axis=1, keepdims=True)
        work = jnp.where(jf == first, -2.0, work)
    sel = valid & (forced | (work == -2.0) | (cur < NSA_TOPN))
    bias = jnp.where(sel, 0.0, NEG_BIG).astype(BF16)
    for st in range(nbp // 64):
        placed = jnp.dot(bias, place_ref[st], preferred_element_type=F32).astype(BF16)
        for g in range(grp):
            qaug_ref[st, g * tq:(g + 1) * tq, :] = q4[g * tq:(g + 1) * tq] + placed

    wlen = NSA_WINDOW + tq
    w0 = pl.multiple_of(jnp.maximum(q0 - NSA_WINDOW, 0), LANES)
    s_w = _dot_nt(q4, kw_ref[0, pl.ds(w0, wlen), :])
    dpos = row_pos((rows, wlen)) - (w0 + lax.broadcasted_iota(jnp.int32, (rows, wlen), 1))
    s_w = jnp.where((dpos >= 0) & (dpos <= NSA_WINDOW), s_w, NEG_BIG)
    e_w = jnp.exp(s_w - jnp.max(s_w, axis=1, keepdims=True))
    l_w = jnp.sum(e_w, axis=1, keepdims=True)
    a_w = jnp.dot(e_w.astype(BF16), vs_ref[0, pl.ds(w0, wlen), :], preferred_element_type=F32) / l_w

    def tile(jt, carry, masked):
        k0 = pl.multiple_of(jt * tk, tk)
        qa = qaug_ref[jt // (NSA_SUPER // tk)]
        s = _dot_nt(qa, ka_ref[0, pl.ds(k0, tk), :])
        if masked:
            kpos = k0 + lax.broadcasted_iota(jnp.int32, (rows, tk), 1)
            s = jnp.where(kpos <= row_pos((rows, tk)), s, NEG_BIG)
        return _online_softmax_step(s, vs_ref[0, pl.ds(k0, tk), :], *carry)

    init = (jnp.full((rows, LANES), NEG_BIG, F32), jnp.zeros((rows, LANES), F32), jnp.zeros((rows, LANES), F32))
    jd = q0 // tk
    carry = lax.fori_loop(0, jd, lambda jt, c: tile(jt, c, False), init)
    _, l_s, acc_s = tile(jd, carry, True)
    a_s = acc_s / l_s

    gt = g_ref[0].reshape(rows, 3)
    lane = lax.broadcasted_iota(jnp.int32, (rows, LANES), 1)
    mix = jnp.where(lane < NSA_DH, gt[:, 1:2] * a_s + gt[:, 0:1] * o_c, gt[:, 2:3] * a_w)
    both = mix + pltpu.roll(mix, NSA_DH, axis=1)
    lane_q = lax.broadcasted_iota(jnp.int32, (tq, LANES), 1)
    for pr in range(grp // 2):
        even = both[(2 * pr) * tq:(2 * pr + 1) * tq]
        odd = both[(2 * pr + 1) * tq:(2 * pr + 2) * tq]
        o_ref[:, pr * LANES:(pr + 1) * LANES] = jnp.where(lane_q < NSA_DH, even, odd)


def nsa_prompt_attention(qp, gates, ka, kw, vs, ck, cv):
    hkv, grp, t, _ = qp.shape
    nbp = ck.shape[1]
    tq, tk = NSA_TQ, NSA_TK
    assert t % tk == 0 and tk % tq == 0 and t >= NSA_WINDOW + tq and nbp % LANES == 0 and nbp * NSA_BLK >= t
    nsup = nbp // 64
    blk = np.arange(nbp)
    place = np.zeros((nsup, nbp, LANES), np.float32)
    place[blk // 64, blk, NSA_DH + blk % 64] = 1.0
    return pl.pallas_call(
        functools.partial(_nsa_prompt_body, tq=tq, tk=tk, nbp=nbp),
        grid=(hkv, t // tq),
        in_specs=[
            pl.BlockSpec((1, grp, tq, LANES), lambda h, i: (h, 0, i, 0)),
            pl.BlockSpec((1, grp, tq, 3), lambda h, i: (h, 0, i, 0)),
            pl.BlockSpec((1, t, LANES), lambda h, i: (h, 0, 0)),
            pl.BlockSpec((1, t, LANES), lambda h, i: (h, 0, 0)),
            pl.BlockSpec((1, t, LANES), lambda h, i: (h, 0, 0)),
            pl.BlockSpec((1, nbp, LANES), lambda h, i: (h, 0, 0)),
            pl.BlockSpec((1, nbp, LANES), lambda h, i: (h, 0, 0)),
            pl.BlockSpec((nsup, nbp, LANES), lambda h, i: (0, 0, 0)),
        ],
        out_specs=pl.BlockSpec((tq, grp * NSA_DH), lambda h, i: (i, h)),
        out_shape=jax.ShapeDtypeStruct((t, hkv * grp * NSA_DH), F32),
        scratch_shapes=[pltpu.VMEM((nsup, grp * tq, LANES), BF16)],
        compiler_params=pltpu.CompilerParams(
            dimension_semantics=("parallel", "arbitrary"),
            vmem_limit_bytes=V7X_VMEM_LIMIT_BYTES),
        name="nsa_prompt",
    )(qp, gates, ka, kw, vs, ck, cv, jnp.asarray(place, BF16))


def nsa_prompt_pallas(q, kv, gates):
    n, _, _, t, _ = q.shape
    assert n == 1
    nb = t // NSA_BLK
    nbp = -(-nb // LANES) * LANES
    zpad = jnp.zeros((NSA_KV_HEADS, t, LANES - NSA_DH), F32)
    qp = jnp.concatenate([q[0] * NSA_SCALE, jnp.zeros(q.shape[1:4] + (LANES - NSA_DH,), F32)], axis=-1).astype(BF16)
    code = (jnp.arange(t)[:, None] // NSA_BLK) % 64 == jnp.arange(64)[None, :]
    ka = jnp.concatenate([kv[0, 1, 0], jnp.broadcast_to(code.astype(F32), (NSA_KV_HEADS, t, 64))], axis=-1).astype(BF16)
    kw = jnp.concatenate([kv[0, 2, 0], zpad], axis=-1).astype(BF16)
    vs = jnp.concatenate([kv[0, 1, 1], kv[0, 2, 1]], axis=-1).astype(BF16)
    means = kv[0, 0].reshape(2, NSA_KV_HEADS, nb, NSA_BLK, NSA_DH).mean(axis=3, dtype=F32)
    means = jnp.pad(means, ((0, 0), (0, 0), (0, nbp - nb), (0, LANES - NSA_DH))).astype(BF16)
    o = nsa_prompt_attention(qp, gates[0], ka, kw, vs, means[0], means[1])
    return o.reshape(n, t, NSA_HEADS * NSA_DH), kv[:, 2, :, :, t - min(NSA_WINDOW, t):]


MLA_DEC_PAGES = 16


def _gather_pages_two_slot(pt_ref, streams, sems, npg):
    ngrp = pl.num_programs(1)
    t = pl.program_id(0) * ngrp + pl.program_id(1)
    slot = lax.rem(t, 2)

    def page_copy(kind, sl, s, page):
        src, buf = streams[kind]
        return pltpu.make_async_copy(src.at[page], buf.at[sl, s], sems.at[kind, sl])

    def start_step(tt, sl):
        for s in range(npg):
            page = pt_ref[tt * npg + s]
            for kind in range(len(streams)):
                page_copy(kind, sl, s, page).start()

    @pl.when(t == 0)
    def _():
        start_step(t, slot)

    @pl.when(t + 1 < pl.num_programs(0) * ngrp)
    def _():
        start_step(t + 1, 1 - slot)

    for s in range(npg):
        for kind in range(len(streams)):
            page_copy(kind, slot, s, 0).wait()
    return slot


def _mla_decode_body(pt_ref, qlat_ref, qpe_ref, cnew_ref, kpenew_ref, ksnew_ref, wuv_ref, ckv_hbm, kpe_hbm, ks_hbm,
                     o_ref, ckv_buf, kpe_buf, ks_buf, sems, m_ref, l_ref, acc_ref, *, npg):
    g = pl.program_id(1)
    slot = _gather_pages_two_slot(pt_ref, ((ckv_hbm, ckv_buf), (kpe_hbm, kpe_buf), (ks_hbm, ks_buf)), sems, npg)

    @pl.when(g == 0)
    def _():
        m_ref[...] = jnp.full(m_ref.shape, NEG_BIG, F32)
        l_ref[...] = jnp.zeros(l_ref.shape, F32)
        acc_ref[...] = jnp.zeros(acc_ref.shape, F32)

    cs = ckv_buf[slot].astype(BF16)
    kpes = kpe_buf[slot].astype(BF16)
    kss = ks_buf[slot]
    qlat = jnp.broadcast_to(qlat_ref[0].astype(BF16)[None], (npg,) + qlat_ref.shape[1:])
    qpe = jnp.broadcast_to(qpe_ref[0].astype(BF16)[None], (npg,) + qpe_ref.shape[1:])
    content = jnp.einsum('bhc,bkc->bhk', qlat, cs, preferred_element_type=F32)
    pe = jnp.einsum('bhr,brk->bhk', qpe, kpes, preferred_element_type=F32)
    sc = (content * kss + pe) * MLA_SCALE
    m_old = m_ref[...]
    m_new = jnp.maximum(m_old, jnp.max(jnp.max(sc, axis=0), axis=1, keepdims=True))
    alpha = jnp.exp(m_old - m_new)
    p32 = jnp.exp(sc - m_new[None])
    l_new = alpha * l_ref[...] + jnp.sum(jnp.sum(p32, axis=0), axis=1, keepdims=True)
    pv = jnp.sum(jnp.einsum('bhk,bkc->bhc', p32.astype(BF16), cs, preferred_element_type=F32), axis=0)
    acc_new = _lane_tile(alpha, MLA_KV_LORA // LANES) * acc_ref[...] + pv
    m_ref[...] = m_new
    l_ref[...] = l_new
    acc_ref[...] = acc_new

    @pl.when(g == pl.num_programs(1) - 1)
    def _():
        cn = _bf16_round(cnew_ref[0])
        content = jnp.sum(_bf16_round(qlat_ref[0]) * cn, axis=1, keepdims=True)
        pe = jnp.sum(_bf16_round(qpe_ref[0]) * _bf16_round(kpenew_ref[0]), axis=1, keepdims=True)
        sc = (content * _col_from_row(ksnew_ref[0]) + pe) * MLA_SCALE
        m_fin = jnp.maximum(m_new, sc)
        a = jnp.exp(m_new - m_fin)
        pn = jnp.exp(sc - m_fin)
        l_fin = a * l_new + pn
        acc = _lane_tile(a, MLA_KV_LORA // LANES) * acc_new + _bf16_round(pn[:, 0:1]) * cn
        o_lat = acc / _lane_tile(l_fin, MLA_KV_LORA // LANES)
        full = jnp.dot(o_lat.astype(BF16), wuv_ref[...], preferred_element_type=F32)
        r = lax.broadcasted_iota(jnp.int32, full.shape, 0)
        cidx = lax.broadcasted_iota(jnp.int32, full.shape, 1)
        o_ref[0] = jnp.sum(jnp.where(r == cidx // MLA_V, full, 0.0), axis=0, keepdims=True)


def mla_decode(q_lat, q_pe, c_new, kpe_new, ks_new, w_uv, cache_ckv, cache_kpe_t, cache_ks_t, page_table):
    n, nh, cdim = q_lat.shape
    rdim = q_pe.shape[2]
    npages = page_table.shape[1]
    npg = MLA_DEC_PAGES
    assert npages % npg == 0
    per_n3 = lambda i, g, pt: (i, 0, 0)
    hbm = pl.BlockSpec(memory_space=pl.ANY)
    in_specs = [
        pl.BlockSpec((1, nh, cdim), per_n3),
        pl.BlockSpec((1, nh, rdim), per_n3),
        pl.BlockSpec((1, 1, cdim), per_n3),
        pl.BlockSpec((1, 1, rdim), per_n3),
        pl.BlockSpec((1, 1, nh), per_n3),
        pl.BlockSpec(w_uv.shape, lambda i, g, pt: (0, 0)),
        hbm, hbm, hbm,
    ]
    grid_spec = pltpu.PrefetchScalarGridSpec(
        num_scalar_prefetch=1, grid=(n, npages // npg), in_specs=in_specs,
        out_specs=pl.BlockSpec((1, 1, w_uv.shape[1]), per_n3),
        scratch_shapes=[pltpu.VMEM((2, npg, PAGE_SIZE, cdim), F32), pltpu.VMEM((2, npg, rdim, PAGE_SIZE), F32),
                        pltpu.VMEM((2, npg, nh, PAGE_SIZE), F32), pltpu.SemaphoreType.DMA((3, 2)),
                        pltpu.VMEM((nh, LANES), F32), pltpu.VMEM((nh, LANES), F32), pltpu.VMEM((nh, cdim), F32)])
    return pl.pallas_call(
        functools.partial(_mla_decode_body, npg=npg),
        grid_spec=grid_spec,
        out_shape=jax.ShapeDtypeStruct((n, 1, w_uv.shape[1]), F32),
        compiler_params=pltpu.CompilerParams(
            dimension_semantics=("arbitrary", "arbitrary"),
            vmem_limit_bytes=V7X_VMEM_LIMIT_BYTES),
        name="mla_decode",
    )(page_table.reshape(-1), q_lat, q_pe, c_new, kpe_new, ks_new, w_uv, cache_ckv, cache_kpe_t, cache_ks_t)


NSA_DEC_PAGES = 16
NSA_DEC_ROWS = 8
NSA_DEC_PICK = NSA_TOPN - NSA_LOCAL - 1
NSA_DEC_NSEL = NSA_TOPN - 1


def _nsa_dec_select_body(pt_ref, q_ref, place_ref, cmp_hbm, oc_ref, idx_ref, parts_ref, page_buf, sems, *, npg, ngrp):
    g = pl.program_id(1)
    slot = _gather_pages_two_slot(pt_ref, ((cmp_hbm, page_buf),), sems, npg)
    bpp = PAGE_SIZE // NSA_BLK
    lane = lax.broadcasted_iota(jnp.int32, (NSA_DH, LANES), 1)
    blks = [jnp.zeros((NSA_DH, LANES), F32) for _ in range(4)]
    for s in range(npg):
        for kv in range(2):
            for h in range(NSA_KV_HEADS):
                slab = page_buf[slot, s, kv, h]
                for b in range(bpp):
                    inblk = (lane >= b * NSA_BLK) & (lane < (b + 1) * NSA_BLK)
                    mean = jnp.sum(jnp.where(inblk, slab, 0.0), axis=1, keepdims=True) * (1.0 / NSA_BLK)
                    t = kv * NSA_KV_HEADS + h
                    blks[t] = jnp.where(lane == s * bpp + b, mean, blks[t])
    for t in range(4):
        parts_ref[g, t] = blks[t]

    @pl.when(g == ngrp - 1)
    def _():
        nb = ngrp * npg * bpp
        nkv = NSA_KV_HEADS
        means_t = [jnp.dot(jnp.concatenate([parts_ref[gg, t] for gg in range(ngrp)], axis=1).astype(BF16),
                           place_ref[...], preferred_element_type=F32).astype(BF16) for t in range(2 * nkv)]
        ck_t = jnp.stack(means_t[:nkv])
        cv_t = jnp.stack(means_t[nkv:])
        q = (q_ref[0] * NSA_SCALE).astype(BF16)
        s_c = jnp.einsum('hqd,hdb->hqb', q, ck_t, preferred_element_type=F32)
        e_c = jnp.exp(s_c - jnp.max(s_c, axis=2, keepdims=True))
        p_c = e_c / jnp.sum(e_c, axis=2, keepdims=True)
        oc_ref[0] = jnp.einsum('hqb,hdb->hqd', p_c.astype(BF16), cv_t, preferred_element_type=F32)
        row = lax.broadcasted_iota(jnp.int32, p_c.shape, 1)
        imp = jnp.sum(jnp.where(row < NSA_GROUP, p_c, 0.0), axis=1)
        jl = lax.broadcasted_iota(jnp.int32, (nb, nb), 1)
        js = lax.broadcasted_iota(jnp.int32, (nb, nb), 0)
        jrow = lax.broadcasted_iota(jnp.int32, (1, nb), 1)
        slot_id = lax.broadcasted_iota(jnp.int32, (nb, LANES), 1)
        jsub = lax.broadcasted_iota(jnp.int32, (nb, LANES), 0)
        slot_row = lax.broadcasted_iota(jnp.int32, (1, LANES), 1)
        for h in range(nkv):
            x_row = jnp.where((jrow >= 1) & (jrow <= nb - NSA_LOCAL), imp[h:h + 1], -1.0)
            x_col = _col_from_row(x_row)
            beaten = (x_row > x_col) | ((x_row == x_col) & (jl < js))
            rank = jnp.sum(beaten.astype(jnp.int32), axis=1, keepdims=True)
            picks = jnp.sum(jnp.where((rank == slot_id) & (slot_id < NSA_DEC_PICK), jsub, 0), axis=0, keepdims=True)
            idx_ref[0, h:h + 1, :] = jnp.where(slot_row == NSA_DEC_PICK + 1, nb - 1, picks)


def nsa_decode_select(q8, cache_cmp_t, page_table):
    n = q8.shape[0]
    npages = page_table.shape[1]
    npg = NSA_DEC_PAGES
    bpp = PAGE_SIZE // NSA_BLK
    assert npages % npg == 0 and npg * bpp <= LANES
    ngrp = npages // npg
    nb = npages * bpp
    assert nb - NSA_LOCAL >= NSA_DEC_PICK + 1
    place = np.zeros((ngrp * LANES, nb), np.float32)
    c = np.arange(npg * bpp)
    for gg in range(ngrp):
        place[gg * LANES + c, gg * npg * bpp + c] = 1.0
    in_specs = [pl.BlockSpec((1, NSA_KV_HEADS, NSA_DEC_ROWS, NSA_DH), lambda i, g, pt: (i, 0, 0, 0)),
                pl.BlockSpec((ngrp * LANES, nb), lambda i, g, pt: (0, 0)),
                pl.BlockSpec(memory_space=pl.ANY)]
    grid_spec = pltpu.PrefetchScalarGridSpec(
        num_scalar_prefetch=1, grid=(n, ngrp), in_specs=in_specs,
        out_specs=[pl.BlockSpec((1, NSA_KV_HEADS, NSA_DEC_ROWS, NSA_DH), lambda i, g, pt: (i, 0, 0, 0)),
                   pl.BlockSpec((1, NSA_KV_HEADS, LANES), lambda i, g, pt: (i, 0, 0))],
        scratch_shapes=[pltpu.VMEM((ngrp, 2 * NSA_KV_HEADS, NSA_DH, LANES), F32),
                        pltpu.VMEM((2, npg, 2, NSA_KV_HEADS, NSA_DH, PAGE_SIZE), F32),
                        pltpu.SemaphoreType.DMA((1, 2))])
    return pl.pallas_call(
        functools.partial(_nsa_dec_select_body, npg=npg, ngrp=ngrp),
        grid_spec=grid_spec,
        out_shape=[jax.ShapeDtypeStruct((n, NSA_KV_HEADS, NSA_DEC_ROWS, NSA_DH), F32),
                   jax.ShapeDtypeStruct((n, NSA_KV_HEADS, LANES), jnp.int32)],
        compiler_params=pltpu.CompilerParams(
            dimension_semantics=("arbitrary", "arbitrary"),
            vmem_limit_bytes=V7X_VMEM_LIMIT_BYTES),
        name="nsa_decode_select",
    )(page_table.reshape(-1), q8, jnp.asarray(place, BF16), cache_cmp_t)


def _nsa_dec_attend_body(pg_ref, half_ref, q_ref, gate_ref, oc_ref, new_ref, *refs, nsel):
    sel_refs = refs[:nsel]
    win_ref, o_ref, wout_ref = refs[nsel:]
    i, h = pl.program_id(0), pl.program_id(1)
    qf = _bf16_round(q_ref[0, 0] * NSA_SCALE)
    q = qf.astype(BF16)
    new = new_ref[0, 0]
    lane = lax.broadcasted_iota(jnp.int32, (NSA_DEC_ROWS, PAGE_SIZE), 1)

    def attend(score_tiles, value_tiles, k_new, v_new):
        s_new = jnp.sum(qf * _bf16_round(k_new), axis=1, keepdims=True)
        sall = jnp.concatenate(score_tiles, axis=1) if len(score_tiles) > 1 else score_tiles[0]
        m = jnp.maximum(jnp.max(sall, axis=1, keepdims=True), s_new)
        p = jnp.exp(sall - m)
        p_new = jnp.exp(s_new - m)
        l = jnp.sum(p, axis=1, keepdims=True) + p_new
        pb = p.astype(BF16)
        acc = _bf16_round(p_new) * _bf16_round(v_new)
        off = 0
        for vt in value_tiles:
            w = vt.shape[1]
            acc = acc + _dot_nt(pb[:, off:off + w], vt)
            off += w
        return acc / l

    s_tiles, v_tiles = [], []
    for b in range(nsel):
        hb = half_ref[(i * NSA_KV_HEADS + h) * nsel + b]
        sb = jnp.dot(q, sel_refs[b][0, 0, 0].astype(BF16), preferred_element_type=F32)
        s_tiles.append(jnp.where(lax.shift_right_logical(lane, 6) == hb, sb, NEG_BIG))
        v_tiles.append(sel_refs[b][0, 1, 0].astype(BF16))
    o_s = attend(s_tiles, v_tiles, new[0:1], new[1:2])
    s_w = jnp.dot(q, win_ref[0, 0, 0].astype(BF16), preferred_element_type=F32)
    o_w = attend([s_w], [win_ref[0, 1, 0].astype(BF16)], new[2:3], new[3:4])
    gt = gate_ref[0, 0]
    o_ref[0, 0] = gt[:, 0:1] * oc_ref[0, 0] + gt[:, 1:2] * o_s + gt[:, 2:3] * o_w

    wl = win_ref.shape[4]
    wlane = lax.broadcasted_iota(jnp.int32, (NSA_DH, wl), 1)
    for kv in range(2):
        col = _col_from_row(new[2 + kv:3 + kv])
        wout_ref[0, kv, 0] = jnp.where(wlane == wl - 1, col, pltpu.roll(win_ref[0, kv, 0], wl - 1, axis=1))


def nsa_decode_attend(pages, halves, q8, gates8, oc8, new_rows, cache_sel_t, win_t):
    n = q8.shape[0]
    nsel = NSA_DEC_NSEL
    wl = win_t.shape[4]
    per = lambda i, h, pg, hf: (i, h, 0, 0)

    def sel_map(b):
        return lambda i, h, pg, hf: (pg[(i * NSA_KV_HEADS + h) * nsel + b], 0, h, 0, 0)

    in_specs = [pl.BlockSpec((1, 1, NSA_DEC_ROWS, NSA_DH), per),
                pl.BlockSpec((1, 1, NSA_DEC_ROWS, 3), per),
                pl.BlockSpec((1, 1, NSA_DEC_ROWS, NSA_DH), per),
                pl.BlockSpec((1, 1, 4, NSA_DH), per)]
    in_specs += [pl.BlockSpec((1, 2, 1, NSA_DH, PAGE_SIZE), sel_map(b)) for b in range(nsel)]
    in_specs += [pl.BlockSpec((1, 2, 1, NSA_DH, wl), lambda i, h, pg, hf: (i, 0, h, 0, 0))]
    grid_spec = pltpu.PrefetchScalarGridSpec(
        num_scalar_prefetch=2, grid=(n, NSA_KV_HEADS), in_specs=in_specs,
        out_specs=[pl.BlockSpec((1, 1, NSA_DEC_ROWS, NSA_DH), per),
                   pl.BlockSpec((1, 2, 1, NSA_DH, wl), lambda i, h, pg, hf: (i, 0, h, 0, 0))])
    return pl.pallas_call(
        functools.partial(_nsa_dec_attend_body, nsel=nsel),
        grid_spec=grid_spec,
        out_shape=[jax.ShapeDtypeStruct((n, NSA_KV_HEADS, NSA_DEC_ROWS, NSA_DH), F32),
                   jax.ShapeDtypeStruct(win_t.shape, F32)],
        compiler_params=pltpu.CompilerParams(
            dimension_semantics=("parallel", "arbitrary"),
            vmem_limit_bytes=V7X_VMEM_LIMIT_BYTES),
        name="nsa_decode_attend",
    )(pages, halves, q8, gates8, oc8, new_rows, *([cache_sel_t] * nsel), win_t)


def nsa_sample_pallas(q, kv, gates, cache_cmp, cache_sel, win_state, page_table):
    n = q.shape[0]
    assert q.shape[3] == 1 and win_state.shape[3] == NSA_WINDOW
    bpp = PAGE_SIZE // NSA_BLK
    padrows = ((0, 0), (0, 0), (0, NSA_DEC_ROWS - NSA_GROUP), (0, 0))
    q8 = jnp.pad(q[:, :, :, 0], padrows)
    gates8 = jnp.pad(gates[:, :, :, 0], padrows)
    feature_major = (0, 1, 2, 4, 3)
    oc8, picks = nsa_decode_select(q8, jnp.transpose(cache_cmp, feature_major), page_table)
    idx = picks[:, :, :NSA_DEC_NSEL]
    pages = jnp.take_along_axis(page_table[:, None, :], idx // bpp, axis=2)
    new_rows = jnp.stack([kv[:, 1, 0, :, 0], kv[:, 1, 1, :, 0], kv[:, 2, 0, :, 0], kv[:, 2, 1, :, 0]], axis=2)
    o8, win_t = nsa_decode_attend(pages.reshape(-1), (idx % bpp).reshape(-1), q8, gates8, oc8, new_rows,
                                  jnp.transpose(cache_sel, feature_major), jnp.transpose(win_state, feature_major))
    o = o8[:, :, :NSA_GROUP].reshape(n, 1, NSA_HEADS * NSA_DH)
    return o, jnp.transpose(win_t, feature_major)


def split_cols(x, sizes):
    return jnp.split(x, np.cumsum(sizes)[:-1].tolist(), axis=-1)


def rms_norm(x, g):
    xf = x.astype(F32)
    y = xf * lax.rsqrt(jnp.mean(xf * xf, axis=-1, keepdims=True) + EPS)
    return (y * g.astype(F32)).astype(x.dtype)


def layer_norm(x, g, b):
    xf = x.astype(F32)
    xc = xf - jnp.mean(xf, axis=-1, keepdims=True)
    y = xc * lax.rsqrt(jnp.mean(xc * xc, axis=-1, keepdims=True) + EPS)
    return (y * g.astype(F32) + b.astype(F32)).astype(x.dtype)


def l2_normalize(x):
    xf = x.astype(F32)
    return xf * lax.rsqrt(jnp.sum(xf * xf, axis=-1, keepdims=True) + EPS)


def rope(x, pos):
    d = x.shape[-1]
    inv = ROPE_THETA ** (-jnp.arange(0, d, 2, dtype=F32) / d)
    ang = pos.astype(F32)[..., None] * inv
    cos, sin = jnp.cos(ang), jnp.sin(ang)
    xf = x.astype(F32)
    x1, x2 = xf[..., : d // 2], xf[..., d // 2:]
    return jnp.concatenate([x1 * cos - x2 * sin, x2 * cos + x1 * sin], axis=-1).astype(x.dtype)


def partial_rope(x, pos, rot):
    return jnp.concatenate([rope(x[..., :rot], pos), x[..., rot:]], axis=-1)


def causal_conv(x_ext, w):
    t = x_ext.shape[1] - (GDN_CONV - 1)
    return sum(x_ext[:, i:i + t] * w[i] for i in range(GDN_CONV))


def gdn_prepare(conv_out, b_raw, a_raw, a_log, dt_bias):
    n, t, _ = conv_out.shape
    q, k, v = split_cols(conv_out, (GDN_HEADS * GDN_DK, GDN_HEADS * GDN_DK, GDN_HEADS * GDN_DV))
    q = l2_normalize(q.reshape(n, t, GDN_HEADS, GDN_DK)) * (GDN_DK ** -0.5)
    k = l2_normalize(k.reshape(n, t, GDN_HEADS, GDN_DK))
    v = v.reshape(n, t, GDN_HEADS, GDN_DV).astype(F32)
    beta = jax.nn.sigmoid(b_raw.astype(F32))
    g = -jnp.exp(a_log.astype(F32)) * jax.nn.softplus(a_raw.astype(F32) + dt_bias.astype(F32))
    return q, k, v, beta, g


GDN_SUPER = 256


def _bmm(a, b):
    return jnp.einsum('hij,hjk->hik', a, b, preferred_element_type=F32)


def _gdn_chunk_body(q_ref, k_ref, kt_ref, v_ref, b_ref, g_ref, o_ref, sfin_ref, s_ref, vnew_ref):
    n, c, nh = GDN_SUPER, GDN_CHUNK, GDN_HEADS

    @pl.when(pl.program_id(0) == 0)
    def _():
        s_ref[...] = jnp.zeros(s_ref.shape, F32)
        vnew_ref[...] = jnp.zeros(vnew_ref.shape, BF16)

    r = lax.broadcasted_iota(jnp.int32, (n, n), 0)
    cc = lax.broadcasted_iota(jnp.int32, (n, n), 1)
    same = lax.shift_right_logical(r, 6) == lax.shift_right_logical(cc, 6)
    tril = same & (cc <= r)
    stril = same & (cc < r)
    last = same & ((cc & (c - 1)) == c - 1)
    eye = (cc == r).astype(F32)
    gc_all = jnp.dot(tril.astype(F32), g_ref[...], precision=lax.Precision.HIGHEST, preferred_element_type=F32)
    b_all = b_ref[...]
    q, k, v = q_ref[...], k_ref[...], v_ref[...]
    ms, decays, vbs, kbgs, qgs, kdts, gls = [], [], [], [], [], [], []
    for h in range(nh):
        gc = gc_all[:, h:h + 1]
        beta = b_all[:, h:h + 1]
        gc_row = jnp.sum(jnp.where(cc == r, jnp.broadcast_to(gc, (n, n)), 0.0), axis=0, keepdims=True)
        glast_col = jnp.sum(jnp.where(last, jnp.broadcast_to(gc_row, (n, n)), 0.0), axis=1, keepdims=True)
        glast_row = jnp.sum(jnp.where(cc == r, jnp.broadcast_to(glast_col, (n, n)), 0.0), axis=0, keepdims=True)
        decays.append(jnp.where(tril, jnp.exp(jnp.where(tril, gc - gc_row, 0.0)), 0.0))
        eg = jnp.exp(gc)
        kb = k[h] * beta
        ms.append(kb.astype(BF16))
        vbs.append((v[h] * beta).astype(BF16))
        kbgs.append((kb * eg).astype(BF16))
        qgs.append((q[h] * eg).astype(BF16))
        kdts.append((kt_ref[h] * jnp.exp(glast_row - gc_row)).astype(BF16))
        gls.append(jnp.exp(glast_col))
    decay = jnp.stack(decays)
    kbf = k.astype(BF16)
    kkt = jnp.einsum('hid,hjd->hij', jnp.stack(ms), kbf, preferred_element_type=F32)
    m = jnp.where(stril[None], kkt * decay, 0.0)
    attn = jnp.where(tril[None], jnp.einsum('hid,hjd->hij', q.astype(BF16), kbf, preferred_element_type=F32) * decay,
                     0.0).astype(BF16)
    tinv = eye[None] - m
    pw = m
    for _ in range(int(np.log2(c)) - 1):
        pwb = pw.astype(BF16)
        pw = _bmm(pwb, pwb)
        tinv = tinv + _bmm(tinv.astype(BF16), pw.astype(BF16))
    tb = tinv.astype(BF16)
    u = _bmm(tb, jnp.stack(vbs))
    w = _bmm(tb, jnp.stack(kbgs)).astype(BF16)
    qg = jnp.stack(qgs)
    kdt = jnp.stack(kdts)
    gl = jnp.stack(gls)
    state = s_ref[...]
    rown = lax.broadcasted_iota(jnp.int32, (n, GDN_DV), 0)
    for ci in range(n // c):
        rows = slice(ci * c, (ci + 1) * c)
        sb = state.astype(BF16)
        vb = (u[:, rows] - _bmm(w[:, rows], sb)).astype(BF16)
        vnew_ref[:, rows, :] = vb
        vall = vnew_ref[...]
        o_ref[:, rows, :] = _bmm(qg[:, rows], sb) + _bmm(attn[:, rows], vall)
        vonly = jnp.where((lax.shift_right_logical(rown, 6) == ci)[None], vall, jnp.zeros_like(vall))
        state = state * gl[:, ci * c:ci * c + 1] + _bmm(kdt, vonly)
    s_ref[...] = state
    sfin_ref[...] = state


def gdn_chunked_pallas(q, k, kt, v, beta, g):
    nh, t, _ = q.shape
    n = GDN_SUPER
    assert t % n == 0 and GDN_DK == GDN_DV and n % GDN_CHUNK == 0 and GDN_CHUNK == 64
    hrow = lambda i: (0, i, 0)
    return pl.pallas_call(
        _gdn_chunk_body,
        grid=(t // n,),
        in_specs=[pl.BlockSpec((nh, n, GDN_DK), hrow), pl.BlockSpec((nh, n, GDN_DK), hrow),
                  pl.BlockSpec((nh, GDN_DK, n), lambda i: (0, 0, i)), pl.BlockSpec((nh, n, GDN_DV), hrow),
                  pl.BlockSpec((n, nh), lambda i: (i, 0)), pl.BlockSpec((n, nh), lambda i: (i, 0))],
        out_specs=[pl.BlockSpec((nh, n, GDN_DV), hrow), pl.BlockSpec((nh, GDN_DK, GDN_DV), lambda i: (0, 0, 0))],
        out_shape=[jax.ShapeDtypeStruct((nh, t, GDN_DV), F32), jax.ShapeDtypeStruct((nh, GDN_DK, GDN_DV), F32)],
        scratch_shapes=[pltpu.VMEM((nh, GDN_DK, GDN_DV), F32), pltpu.VMEM((nh, n, GDN_DV), BF16)],
        compiler_params=pltpu.CompilerParams(
            dimension_semantics=("arbitrary",),
            vmem_limit_bytes=V7X_VMEM_LIMIT_BYTES),
        name="gdn_chunked",
    )(q, k, kt, v, beta, g)


def gdn_recurrent(q, k, v, beta, g, s0):
    def step(s, xs):
        q_t, k_t, v_t, b_t, g_t = xs
        s = s * jnp.exp(g_t)[..., None, None]
        delta = (v_t - jnp.einsum('nhde,nhd->nhe', s, k_t)) * b_t[..., None]
        s = s + jnp.einsum('nhd,nhe->nhde', k_t, delta)
        return s, jnp.einsum('nhde,nhd->nhe', s, q_t)

    xs = tuple(jnp.moveaxis(x, 1, 0) for x in (q, k, v, beta, g))
    s_fin, o = lax.scan(step, s0, xs)
    return jnp.moveaxis(o, 0, 1), s_fin


def gdn_output(o, z, out_norm):
    n, t = z.shape[:2]
    gate = jax.nn.silu(z.reshape(n, t, GDN_HEADS, GDN_DV).astype(F32))
    return (rms_norm(o, out_norm) * gate).astype(z.dtype).reshape(n, t, GDN_HEADS * GDN_DV)


def even_project(xn, pos, w_in, q_norm, k_norm):
    n, t, _ = xn.shape
    qkv, z, b_raw, a_raw, q, kv, g_raw = split_cols(xn @ w_in, EVEN_SIZES)
    q = partial_rope(rms_norm(q.reshape(n, t, NSA_HEADS, NSA_DH), q_norm), pos[:, None], NSA_ROT)
    q = jnp.transpose(q.reshape(n, t, NSA_KV_HEADS, NSA_GROUP, NSA_DH), (0, 2, 3, 1, 4))
    kv = kv.reshape(n, t, 3, 2, NSA_KV_HEADS, NSA_DH)
    k = partial_rope(rms_norm(kv[:, :, :, 0], k_norm[:, None, :]), pos[:, None, None], NSA_ROT)
    kv = jnp.transpose(jnp.stack([k, kv[:, :, :, 1]], axis=3), (0, 2, 3, 4, 1, 5))
    gates = jax.nn.sigmoid(g_raw.astype(F32)).reshape(n, t, NSA_KV_HEADS, NSA_GROUP, 3)
    return qkv, z, b_raw, a_raw, q, kv, jnp.transpose(gates, (0, 2, 3, 1, 4))


def even_prompt(xn, pos, w_in, w_out, conv_w, a_log, dt_bias, out_norm, q_norm, k_norm):
    n = xn.shape[0]
    qkv, z, b_raw, a_raw, q, kv, gates = even_project(xn, pos, w_in, q_norm, k_norm)
    qkv_ext = jnp.concatenate([jnp.zeros((n, GDN_CONV - 1, GDN_CONV_CH), qkv.dtype), qkv], axis=1)
    gq, gk, gv, beta, g = gdn_prepare(jax.nn.silu(causal_conv(qkv_ext, conv_w)), b_raw, a_raw, a_log, dt_bias)
    assert n == 1
    qh, kh, vh = (jnp.transpose(x[0], (1, 0, 2)) for x in (gq, gk, gv))
    o_h, s_fin = gdn_chunked_pallas(qh, kh, jnp.transpose(kh, (0, 2, 1)), vh, beta[0], g[0])
    o, s_fin = jnp.transpose(o_h, (1, 0, 2))[None], s_fin[None]
    nsa_out, win = nsa_prompt_pallas(q, kv, gates)
    y = jnp.concatenate([gdn_output(o, z, out_norm), nsa_out], axis=-1) @ w_out
    return y, (s_fin.astype(xn.dtype), qkv_ext[:, -(GDN_CONV - 1):], kv[:, 0], kv[:, 1], win)


def even_sample(xn, pos, gdn_state, conv_state, cache_cmp, cache_sel, win_state, page_table,
                w_in, w_out, conv_w, a_log, dt_bias, out_norm, q_norm, k_norm):
    qkv, z, b_raw, a_raw, q, kv, gates = even_project(xn, pos, w_in, q_norm, k_norm)
    qkv_ext = jnp.concatenate([conv_state.astype(qkv.dtype), qkv], axis=1)
    gq, gk, gv, beta, g = gdn_prepare(jax.nn.silu(causal_conv(qkv_ext, conv_w)), b_raw, a_raw, a_log, dt_bias)
    o, s_fin = gdn_recurrent(gq, gk, gv, beta, g, gdn_state.astype(F32))
    nsa_out, win = nsa_sample_pallas(q, kv, gates, cache_cmp, cache_sel, win_state, page_table)
    y = jnp.concatenate([gdn_output(o, z, out_norm), nsa_out], axis=-1) @ w_out
    return y, (s_fin.astype(xn.dtype), qkv_ext[:, -(GDN_CONV - 1):], kv[:, 0], kv[:, 1], win)


def sgu_mix(u, v, w_s, b_s):
    n, t, _ = u.shape
    l = min(t, SGU_CHUNK)
    nc = t // l
    w = jnp.tril(w_s[:, :l, :l])
    mix = jnp.einsum('gts,ncsgd->nctgd', w, v.reshape(n, nc, l, SGU_GROUPS, SGU_DG))
    mix = mix + jnp.transpose(b_s[:, :l])[:, :, None]
    return (u.reshape(n, nc, l, SGU_GROUPS, SGU_DG) * mix).reshape(n, t, SGU_WIDTH)


def odd_project(xn, pos, w_in, ln_g, ln_b, cq_norm, ckv_norm, w_uq, w_uk, qn_norm, qr_norm, kr_norm):
    n, t, _ = xn.shape
    uv, cq, ckv, kr = split_cols(xn @ w_in, ODD_SIZES)
    uv = jax.nn.gelu(uv, approximate=False)
    u = uv[..., :SGU_WIDTH]
    v = layer_norm(uv[..., SGU_WIDTH:], ln_g, ln_b)
    q = (rms_norm(cq, cq_norm) @ w_uq).reshape(n, t, MLA_HEADS, MLA_NOPE + MLA_ROPE)
    q_nope = rms_norm(q[..., :MLA_NOPE], qn_norm)
    q_pe = rope(rms_norm(q[..., MLA_NOPE:], qr_norm), pos[:, None])
    c = rms_norm(ckv, ckv_norm)
    k_pe = rope(rms_norm(kr, kr_norm), pos)
    k_nope = jnp.einsum('ntc,chd->nthd', c, w_uk).astype(F32)
    kscale = lax.rsqrt(jnp.mean(k_nope * k_nope, axis=-1) + EPS)
    return u, v, q_nope, q_pe, c, k_pe, k_nope, kscale


def odd_prompt(xn, pos, w_in, w_out, ln_g, ln_b, sgu_w, sgu_b, cq_norm, ckv_norm, w_uq, w_uk, w_uv,
               qn_norm, qr_norm, kn_norm, kr_norm):
    n, t, _ = xn.shape
    u, v, q_nope, q_pe, c, k_pe, k_nope, kscale = odd_project(
        xn, pos, w_in, ln_g, ln_b, cq_norm, ckv_norm, w_uq, w_uk, qn_norm, qr_norm, kr_norm)
    sgu = sgu_mix(u, v, sgu_w, sgu_b)
    k_nope_n = (k_nope * kscale[..., None] * kn_norm.astype(F32)).astype(xn.dtype)
    pad = jnp.zeros((n, t, MLA_HEADS, LANES - MLA_NOPE - MLA_ROPE), F32)
    q = jnp.concatenate([q_nope * MLA_SCALE, q_pe * MLA_SCALE, pad], axis=-1)
    k = jnp.concatenate([k_nope_n, jnp.broadcast_to(k_pe[:, :, None, :], (n, t, MLA_HEADS, MLA_ROPE)), pad], axis=-1)
    vv = jnp.einsum('ntc,chd->nthd', c, w_uv)
    assert n == 1
    att = flash_causal_pairs(q.reshape(t, MLA_HEADS * LANES).astype(BF16),
                             k.reshape(t, MLA_HEADS * LANES).astype(BF16),
                             vv.reshape(t, MLA_HEADS * MLA_V).astype(BF16)).reshape(n, t, MLA_HEADS * MLA_V)
    y = jnp.concatenate([sgu, att], axis=-1) @ w_out
    return y, (c, k_pe, kscale.astype(xn.dtype))


def odd_sample(xn, pos, cache_ckv, cache_kpe, cache_kscale, page_table, w_in, w_out, ln_g, ln_b, sgu_w,
               sgu_b, cq_norm, ckv_norm, w_uq, w_uk, w_uv, qn_norm, qr_norm, kn_norm, kr_norm):
    n, s, _ = xn.shape
    u, v, q_nope, q_pe, c, k_pe, k_nope, kscale = odd_project(
        xn, pos, w_in, ln_g, ln_b, cq_norm, ckv_norm, w_uq, w_uk, qn_norm, qr_norm, kr_norm)
    sgu = sgu_mix(u, v, sgu_w, sgu_b)
    q_lat = jnp.einsum('nqhd,chd->nqhc', q_nope.astype(F32) * kn_norm.astype(F32), w_uk)
    assert s == 1
    att = mla_decode(q_lat[:, 0], q_pe[:, 0], c, k_pe, kscale.astype(F32),
                     w_uv.reshape(MLA_KV_LORA, MLA_HEADS * MLA_V).astype(BF16), cache_ckv,
                     jnp.swapaxes(cache_kpe, 1, 2), jnp.swapaxes(cache_kscale, 1, 2), page_table)
    y = jnp.concatenate([sgu, att], axis=-1) @ w_out
    return y, (c, k_pe, kscale.astype(xn.dtype), v)


def kernel(x_prompt, x_sample, state_gdn, state_gdn_conv, cache_nsa_cmp, cache_nsa_sel, state_nsa_win,
           cache_mla_ckv, cache_mla_kpe, cache_mla_kscale, page_table, norm_mix, norm_ffn, ffn_up, ffn_down,
           w_in_even, w_out_even, gdn_conv_w, gdn_a_log, gdn_dt_bias, gdn_out_norm, nsa_q_norm, nsa_k_norm,
           w_in_odd, w_out_odd, sgu_ln_g, sgu_ln_b, sgu_w, sgu_b, mla_cq_norm, mla_ckv_norm, mla_w_uq,
           mla_w_uk, mla_w_uv, mla_qn_norm, mla_qr_norm, mla_kn_norm, mla_kr_norm):
    pos_p = jnp.arange(x_prompt.shape[1], dtype=jnp.int32)
    pos_s = page_table.shape[1] * PAGE_SIZE + jnp.arange(x_sample.shape[1], dtype=jnp.int32)
    ffn_up_b = ffn_up.astype(BF16)
    ffn_down_b = ffn_down.astype(BF16)
    hp, hs = x_prompt, x_sample
    for layer in range(norm_mix.shape[0]):
        xp = rms_norm(hp, norm_mix[layer])
        xs = rms_norm(hs, norm_mix[layer])
        if layer % 2 == 0:
            mp, (p_gdn_state, p_gdn_conv, p_nsa_cmp, p_nsa_sel, p_nsa_win) = even_prompt(
                xp, pos_p, w_in_even, w_out_even, gdn_conv_w, gdn_a_log, gdn_dt_bias, gdn_out_norm,
                nsa_q_norm, nsa_k_norm)
            ms, (s_gdn_state, s_gdn_conv, s_nsa_cmp, s_nsa_sel, s_nsa_win) = even_sample(
                xs, pos_s, state_gdn, state_gdn_conv, cache_nsa_cmp, cache_nsa_sel, state_nsa_win, page_table,
                w_in_even, w_out_even, gdn_conv_w, gdn_a_log, gdn_dt_bias, gdn_out_norm, nsa_q_norm, nsa_k_norm)
        else:
            mp, (p_mla_ckv, p_mla_kpe, p_mla_kscale) = odd_prompt(
                xp, pos_p, w_in_odd, w_out_odd, sgu_ln_g, sgu_ln_b, sgu_w, sgu_b, mla_cq_norm, mla_ckv_norm,
                mla_w_uq, mla_w_uk, mla_w_uv, mla_qn_norm, mla_qr_norm, mla_kn_norm, mla_kr_norm)
            ms, (s_mla_ckv, s_mla_kpe, s_mla_kscale, s_sgu_v) = odd_sample(
                xs, pos_s, cache_mla_ckv, cache_mla_kpe, cache_mla_kscale, page_table, w_in_odd, w_out_odd,
                sgu_ln_g, sgu_ln_b, sgu_w, sgu_b, mla_cq_norm, mla_ckv_norm, mla_w_uq, mla_w_uk, mla_w_uv,
                mla_qn_norm, mla_qr_norm, mla_kn_norm, mla_kr_norm)
        hp = sq_relu_mlp(hp + mp, norm_ffn[layer], ffn_up_b[layer], ffn_down_b[layer])
        hs = sq_relu_mlp(hs + ms, norm_ffn[layer], ffn_up_b[layer], ffn_down_b[layer])
    return (hp, hs, p_gdn_state, p_gdn_conv, p_nsa_cmp, p_nsa_sel, p_nsa_win, p_mla_ckv, p_mla_kpe, p_mla_kscale,
            s_gdn_state, s_gdn_conv, s_nsa_cmp, s_nsa_sel, s_nsa_win, s_mla_ckv, s_mla_kpe, s_mla_kscale, s_sgu_v)
```

```python
import functools

import jax
import jax.numpy as jnp
import numpy as np
from jax import lax
from jax.experimental import pallas as pl
from jax.experimental.pallas import tpu as pltpu

F32 = jnp.float32
BF16 = jnp.bfloat16

D_MODEL = 1024
PAGE_SIZE = 128
ROPE_THETA = 500000.0
EPS = 1e-6

GDN_HEADS = 4
GDN_DK = 128
GDN_DV = 128
GDN_CONV = 4
GDN_CHUNK = 64
GDN_CONV_CH = GDN_HEADS * (2 * GDN_DK + GDN_DV)

NSA_HEADS = 8
NSA_KV_HEADS = 2
NSA_GROUP = NSA_HEADS // NSA_KV_HEADS
NSA_DH = 64
NSA_ROT = NSA_DH // 4
NSA_BLK = 64
NSA_TOPN = 16
NSA_LOCAL = 2
NSA_WINDOW = 512
NSA_SCALE = NSA_DH ** -0.5

SGU_GROUPS = 4
SGU_DG = 128
SGU_CHUNK = 128
SGU_WIDTH = SGU_GROUPS * SGU_DG

MLA_HEADS = 8
MLA_Q_LORA = 256
MLA_KV_LORA = 256
MLA_NOPE = 64
MLA_ROPE = 32
MLA_V = 64
MLA_SCALE = (MLA_NOPE + MLA_ROPE) ** -0.5

EVEN_SIZES = (GDN_CONV_CH, GDN_HEADS * GDN_DV, GDN_HEADS, GDN_HEADS,
              NSA_HEADS * NSA_DH, 3 * 2 * NSA_KV_HEADS * NSA_DH, 3 * NSA_HEADS)
ODD_SIZES = (2 * SGU_WIDTH, MLA_Q_LORA, MLA_KV_LORA, MLA_ROPE)

V7X_VMEM_LIMIT_BYTES = 56 * 1024 * 1024
LANES = 128
NEG_BIG = -1e30


def _dot_nt(a, b):
    return lax.dot_general(a, b, (((1,), (1,)), ((), ())), preferred_element_type=F32)


def _bf16_round(x):
    return x.astype(BF16).astype(F32)


def _lane_tile(x, k):
    return x if k == 1 else jnp.concatenate([x] * k, axis=1)


def _col_from_row(row):
    k = row.shape[1]
    r = lax.broadcasted_iota(jnp.int32, (k, k), 0)
    c = lax.broadcasted_iota(jnp.int32, (k, k), 1)
    return jnp.sum(jnp.where(r == c, jnp.broadcast_to(row, (k, k)), 0.0), axis=1, keepdims=True)


def _mlp_body(x_ref, g_ref, wu_ref, wd_ref, o_ref, xn_ref, acc_ref):
    f = pl.program_id(1)

    @pl.when(f == 0)
    def _():
        x = x_ref[...]
        y = x * lax.rsqrt(jnp.mean(x * x, axis=-1, keepdims=True) + EPS)
        xn_ref[...] = (y * g_ref[...]).astype(BF16)
        acc_ref[...] = jnp.zeros_like(acc_ref)

    z = jnp.dot(xn_ref[...], wu_ref[...], preferred_element_type=F32)
    a = jnp.square(jnp.maximum(z, 0.0)).astype(BF16)
    acc_ref[...] += jnp.dot(a, wd_ref[...], preferred_element_type=F32)

    @pl.when(f == pl.num_programs(1) - 1)
    def _():
        o_ref[...] = x_ref[...] + acc_ref[...]


def mlp_tiles(t):
    tm = min(t, 1024)
    tf = 512
    return tm, tf


def sq_relu_mlp(h, g, w_up, w_down):
    n, t, d = h.shape
    rows = n * t
    x = h.reshape(rows, d)
    tm, tf = mlp_tiles(rows)
    dff = w_up.shape[1]
    out = pl.pallas_call(
        _mlp_body,
        grid=(rows // tm, dff // tf),
        in_specs=[
            pl.BlockSpec((tm, d), lambda i, f: (i, 0)),
            pl.BlockSpec((1, d), lambda i, f: (0, 0)),
            pl.BlockSpec((d, tf), lambda i, f: (0, f)),
            pl.BlockSpec((tf, d), lambda i, f: (f, 0)),
        ],
        out_specs=pl.BlockSpec((tm, d), lambda i, f: (i, 0)),
        out_shape=jax.ShapeDtypeStruct((rows, d), F32),
        scratch_shapes=[pltpu.VMEM((tm, d), BF16), pltpu.VMEM((tm, d), F32)],
        compiler_params=pltpu.CompilerParams(
            dimension_semantics=("parallel", "arbitrary"),
            vmem_limit_bytes=V7X_VMEM_LIMIT_BYTES),
        name="sq_relu_mlp",
    )(x, g.reshape(1, d), w_up, w_down)
    return out.reshape(n, t, d)


FLASH_TQ = 1024
FLASH_TK = 1024


def _online_softmax_step(s, v, m, l, acc):
    tk = s.shape[1]
    m_new = jnp.maximum(m, jnp.max(s, axis=1, keepdims=True))
    alpha = jnp.exp(m - m_new)
    p = jnp.exp(s - _lane_tile(m_new, tk // LANES))
    l_new = alpha * l + jnp.sum(p, axis=1, keepdims=True)
    pv = jnp.dot(p.astype(BF16), v, preferred_element_type=F32)
    acc_new = _lane_tile(alpha, acc.shape[1] // LANES) * acc + pv
    return m_new, l_new, acc_new


def _flash_pair_body(q_ref, k_ref, v_ref, o_ref, *, tq, tk, dv):
    i = pl.program_id(1)
    q = [q_ref[:, h * LANES:(h + 1) * LANES] for h in range(2)]

    def tile(j, carry, masked):
        k0 = pl.multiple_of(j * tk, tk)
        v = v_ref[pl.ds(k0, tk), :]
        out = []
        for h in range(2):
            k = k_ref[pl.ds(k0, tk), h * LANES:(h + 1) * LANES]
            s = _dot_nt(q[h], k)
            if masked:
                rows = i * tq + lax.broadcasted_iota(jnp.int32, (tq, tk), 0)
                cols = k0 + lax.broadcasted_iota(jnp.int32, (tq, tk), 1)
                s = jnp.where(cols <= rows, s, NEG_BIG)
            out.append(_online_softmax_step(s, v, *carry[h]))
        return tuple(out)

    init = tuple((jnp.full((tq, LANES), NEG_BIG, F32), jnp.zeros((tq, LANES), F32),
                  jnp.zeros((tq, LANES), F32)) for _ in range(2))
    nfull = i * (tq // tk)
    carry = lax.fori_loop(0, nfull, lambda j, c: tile(j, c, False), init)
    for d in range(tq // tk):
        carry = tile(nfull + d, carry, True)
    (_, l0, a0), (_, l1, a1) = carry
    lane = lax.broadcasted_iota(jnp.int32, (tq, LANES), 1)
    o_ref[...] = jnp.where(lane < dv, a0 / l0, a1 / l1)


def flash_causal_pairs(q, k, v):
    t = q.shape[0]
    nh = q.shape[1] // LANES
    dv = v.shape[1] // nh
    assert 2 * dv == LANES and nh % 2 == 0 and t % FLASH_TQ == 0 and FLASH_TQ % FLASH_TK == 0
    return pl.pallas_call(
        functools.partial(_flash_pair_body, tq=FLASH_TQ, tk=FLASH_TK, dv=dv),
        grid=(nh // 2, t // FLASH_TQ),
        in_specs=[
            pl.BlockSpec((FLASH_TQ, 2 * LANES), lambda hp, i: (i, hp)),
            pl.BlockSpec((t, 2 * LANES), lambda hp, i: (0, hp)),
            pl.BlockSpec((t, LANES), lambda hp, i: (0, hp)),
        ],
        out_specs=pl.BlockSpec((FLASH_TQ, LANES), lambda hp, i: (i, hp)),
        out_shape=jax.ShapeDtypeStruct((t, nh * dv), F32),
        compiler_params=pltpu.CompilerParams(
            dimension_semantics=("parallel", "arbitrary"),
            vmem_limit_bytes=V7X_VMEM_LIMIT_BYTES),
        name="mla_flash",
    )(q, k, v)


NSA_TQ = 256
NSA_TK = 1024
NSA_SUPER = 64 * NSA_BLK


def _nsa_prompt_body(q_ref, g_ref, ka_ref, kw_ref, vs_ref, ck_ref, cv_ref, place_ref, o_ref, qaug_ref,
                     *, tq, tk, nbp):
    grp = NSA_GROUP
    rows = grp * tq
    i = pl.program_id(1)
    q0 = i * tq
    q4 = q_ref[0].reshape(rows, LANES)

    def row_pos(shape):
        r = lax.broadcasted_iota(jnp.int32, shape, 0)
        return q0 + (r & (tq - 1))

    s_c = _dot_nt(q4, ck_ref[0])
    jc = lax.broadcasted_iota(jnp.int32, (rows, nbp), 1)
    cmask = (jc + 1) * NSA_BLK <= row_pos((rows, nbp)) + 1
    s_c = jnp.where(cmask, s_c, NEG_BIG)
    e_c = jnp.where(cmask, jnp.exp(s_c - jnp.max(s_c, axis=1, keepdims=True)), 0.0)
    l_c = jnp.sum(e_c, axis=1, keepdims=True)
    p_c = e_c / jnp.maximum(l_c, 1e-30)
    o_c = jnp.dot(p_c.astype(BF16), cv_ref[0], preferred_element_type=F32)
    imp = p_c[0:tq]
    for g in range(1, grp):
        imp = imp + p_c[g * tq:(g + 1) * tq]

    j = lax.broadcasted_iota(jnp.int32, (tq, nbp), 1)
    cur = lax.shift_right_logical(q0 + lax.broadcasted_iota(jnp.int32, (tq, nbp), 0), 6)
    valid = j <= cur
    forced = (j == 0) | (j > cur - NSA_LOCAL)
    work = jnp.where(valid & jnp.logical_not(forced), imp, -1.0)
    jf = j.astype(F32)
    for _ in range(NSA_TOPN - NSA_LOCAL - 1):
        top = jnp.max(work, axis=1, keepdims=True)
        first = jnp.min(jnp.where(work == top, jf, float(nbp)), axis=1, keepdims=True)
        work = jnp.where(jf == first, -2.0, work)
    sel = valid & (forced | (work == -2.0) | (cur < NSA_TOPN))
    bias = jnp.where(sel, 0.0, NEG_BIG).astype(BF16)
    for st in range(nbp // 64):
        placed = jnp.dot(bias, place_ref[st], preferred_element_type=F32).astype(BF16)
        for g in range(grp):
            qaug_ref[st, g * tq:(g + 1) * tq, :] = q4[g * tq:(g + 1) * tq] + placed

    wlen = NSA_WINDOW + tq
    w0 = pl.multiple_of(jnp.maximum(q0 - NSA_WINDOW, 0), LANES)
    s_w = _dot_nt(q4, kw_ref[0, pl.ds(w0, wlen), :])
    dpos = row_pos((rows, wlen)) - (w0 + lax.broadcasted_iota(jnp.int32, (rows, wlen), 1))
    s_w = jnp.where((dpos >= 0) & (dpos <= NSA_WINDOW), s_w, NEG_BIG)
    e_w = jnp.exp(s_w - jnp.max(s_w, axis=1, keepdims=True))
    l_w = jnp.sum(e_w, axis=1, keepdims=True)
    a_w = jnp.dot(e_w.astype(BF16), vs_ref[0, pl.ds(w0, wlen), :], preferred_element_type=F32) / l_w

    def tile(jt, carry, masked):
        k0 = pl.multiple_of(jt * tk, tk)
        qa = qaug_ref[jt // (NSA_SUPER // tk)]
        s = _dot_nt(qa, ka_ref[0, pl.ds(k0, tk), :])
        if masked:
            kpos = k0 + lax.broadcasted_iota(jnp.int32, (rows, tk), 1)
            s = jnp.where(kpos <= row_pos((rows, tk)), s, NEG_BIG)
        return _online_softmax_step(s, vs_ref[0, pl.ds(k0, tk), :], *carry)

    init = (jnp.full((rows, LANES), NEG_BIG, F32), jnp.zeros((rows, LANES), F32), jnp.zeros((rows, LANES), F32))
    jd = q0 // tk
    carry = lax.fori_loop(0, jd, lambda jt, c: tile(jt, c, False), init)
    _, l_s, acc_s = tile(jd, carry, True)
    a_s = acc_s / l_s

    gt = g_ref[0].reshape(rows, 3)
    lane = lax.broadcasted_iota(jnp.int32, (rows, LANES), 1)
    mix = jnp.where(lane < NSA_DH, gt[:, 1:2] * a_s + gt[:, 0:1] * o_c, gt[:, 2:3] * a_w)
    both = mix + pltpu.roll(mix, NSA_DH, axis=1)
    lane_q = lax.broadcasted_iota(jnp.int32, (tq, LANES), 1)
    for pr in range(grp // 2):
        even = both[(2 * pr) * tq:(2 * pr + 1) * tq]
        odd = both[(2 * pr + 1) * tq:(2 * pr + 2) * tq]
        o_ref[:, pr * LANES:(pr + 1) * LANES] = jnp.where(lane_q < NSA_DH, even, odd)


def nsa_prompt_attention(qp, gates, ka, kw, vs, ck, cv):
    hkv, grp, t, _ = qp.shape
    nbp = ck.shape[1]
    tq, tk = NSA_TQ, NSA_TK
    assert t % tk == 0 and tk % tq == 0 and t >= NSA_WINDOW + tq and nbp % LANES == 0 and nbp * NSA_BLK >= t
    nsup = nbp // 64
    blk = np.arange(nbp)
    place = np.zeros((nsup, nbp, LANES), np.float32)
    place[blk // 64, blk, NSA_DH + blk % 64] = 1.0
    return pl.pallas_call(
        functools.partial(_nsa_prompt_body, tq=tq, tk=tk, nbp=nbp),
        grid=(hkv, t // tq),
        in_specs=[
            pl.BlockSpec((1, grp, tq, LANES), lambda h, i: (h, 0, i, 0)),
            pl.BlockSpec((1, grp, tq, 3), lambda h, i: (h, 0, i, 0)),
            pl.BlockSpec((1, t, LANES), lambda h, i: (h, 0, 0)),
            pl.BlockSpec((1, t, LANES), lambda h, i: (h, 0, 0)),
            pl.BlockSpec((1, t, LANES), lambda h, i: (h, 0, 0)),
            pl.BlockSpec((1, nbp, LANES), lambda h, i: (h, 0, 0)),
            pl.BlockSpec((1, nbp, LANES), lambda h, i: (h, 0, 0)),
            pl.BlockSpec((nsup, nbp, LANES), lambda h, i: (0, 0, 0)),
        ],
        out_specs=pl.BlockSpec((tq, grp * NSA_DH), lambda h, i: (i, h)),
        out_shape=jax.ShapeDtypeStruct((t, hkv * grp * NSA_DH), F32),
        scratch_shapes=[pltpu.VMEM((nsup, grp * tq, LANES), BF16)],
        compiler_params=pltpu.CompilerParams(
            dimension_semantics=("parallel", "arbitrary"),
            vmem_limit_bytes=V7X_VMEM_LIMIT_BYTES),
        name="nsa_prompt",
    )(qp, gates, ka, kw, vs, ck, cv, jnp.asarray(place, BF16))


def nsa_prompt_pallas(q, kv, gates):
    n, _, _, t, _ = q.shape
    assert n == 1
    nb = t // NSA_BLK
    nbp = -(-nb // LANES) * LANES
    zpad = jnp.zeros((NSA_KV_HEADS, t, LANES - NSA_DH), F32)
    qp = jnp.concatenate([q[0] * NSA_SCALE, jnp.zeros(q.shape[1:4] + (LANES - NSA_DH,), F32)], axis=-1).astype(BF16)
    code = (jnp.arange(t)[:, None] // NSA_BLK) % 64 == jnp.arange(64)[None, :]
    ka = jnp.concatenate([kv[0, 1, 0], jnp.broadcast_to(code.astype(F32), (NSA_KV_HEADS, t, 64))], axis=-1).astype(BF16)
    kw = jnp.concatenate([kv[0, 2, 0], zpad], axis=-1).astype(BF16)
    vs = jnp.concatenate([kv[0, 1, 1], kv[0, 2, 1]], axis=-1).astype(BF16)
    means = kv[0, 0].reshape(2, NSA_KV_HEADS, nb, NSA_BLK, NSA_DH).mean(axis=3, dtype=F32)
    means = jnp.pad(means, ((0, 0), (0, 0), (0, nbp - nb), (0, LANES - NSA_DH))).astype(BF16)
    o = nsa_prompt_attention(qp, gates[0], ka, kw, vs, means[0], means[1])
    return o.reshape(n, t, NSA_HEADS * NSA_DH), kv[:, 2, :, :, t - min(NSA_WINDOW, t):]


MLA_DEC_PAGES = 16


PAGE_SLOTS = 3


def _gather_pages_two_slot(pt_ref, streams, sems, npg):
    ngrp = pl.num_programs(1)
    total = pl.num_programs(0) * ngrp
    t = pl.program_id(0) * ngrp + pl.program_id(1)
    slot = lax.rem(t, PAGE_SLOTS)
    ahead = PAGE_SLOTS - 1

    def page_copy(kind, sl, s, page):
        src, buf = streams[kind]
        return pltpu.make_async_copy(src.at[page], buf.at[sl, s], sems.at[kind, sl])

    def start_step(tt):
        sl = lax.rem(tt, PAGE_SLOTS)
        for s in range(npg):
            page = pt_ref[tt * npg + s]
            for kind in range(len(streams)):
                page_copy(kind, sl, s, page).start()

    for d in range(ahead):
        @pl.when((t == 0) & (d < total))
        def _(d=d):
            start_step(d)

    @pl.when(t + ahead < total)
    def _():
        start_step(t + ahead)

    for s in range(npg):
        for kind in range(len(streams)):
            page_copy(kind, slot, s, 0).wait()
    return slot


def _mla_decode_body(pt_ref, qlat_ref, qpe_ref, cnew_ref, kpenew_ref, ksnew_ref, wuv_ref, ckv_hbm, kpe_hbm, ks_hbm,
                     o_ref, ckv_buf, kpe_buf, ks_buf, sems, m_ref, l_ref, acc_ref, *, npg):
    g = pl.program_id(1)
    slot = _gather_pages_two_slot(pt_ref, ((ckv_hbm, ckv_buf), (kpe_hbm, kpe_buf), (ks_hbm, ks_buf)), sems, npg)

    @pl.when(g == 0)
    def _():
        m_ref[...] = jnp.full(m_ref.shape, NEG_BIG, F32)
        l_ref[...] = jnp.zeros(l_ref.shape, F32)
        acc_ref[...] = jnp.zeros(acc_ref.shape, F32)

    cs = ckv_buf[slot].astype(BF16)
    kpes = kpe_buf[slot].astype(BF16)
    kss = ks_buf[slot]
    qlat = jnp.broadcast_to(qlat_ref[0].astype(BF16)[None], (npg,) + qlat_ref.shape[1:])
    qpe = jnp.broadcast_to(qpe_ref[0].astype(BF16)[None], (npg,) + qpe_ref.shape[1:])
    content = jnp.einsum('bhc,bkc->bhk', qlat, cs, preferred_element_type=F32)
    pe = jnp.einsum('bhr,brk->bhk', qpe, kpes, preferred_element_type=F32)
    sc = (content * kss + pe) * MLA_SCALE
    m_old = m_ref[...]
    m_new = jnp.maximum(m_old, jnp.max(jnp.max(sc, axis=0), axis=1, keepdims=True))
    alpha = jnp.exp(m_old - m_new)
    p32 = jnp.exp(sc - m_new[None])
    l_new = alpha * l_ref[...] + jnp.sum(jnp.sum(p32, axis=0), axis=1, keepdims=True)
    pv = jnp.sum(jnp.einsum('bhk,bkc->bhc', p32.astype(BF16), cs, preferred_element_type=F32), axis=0)
    acc_new = _lane_tile(alpha, MLA_KV_LORA // LANES) * acc_ref[...] + pv
    m_ref[...] = m_new
    l_ref[...] = l_new
    acc_ref[...] = acc_new

    @pl.when(g == pl.num_programs(1) - 1)
    def _():
        cn = _bf16_round(cnew_ref[0])
        content = jnp.sum(_bf16_round(qlat_ref[0]) * cn, axis=1, keepdims=True)
        pe = jnp.sum(_bf16_round(qpe_ref[0]) * _bf16_round(kpenew_ref[0]), axis=1, keepdims=True)
        sc = (content * _col_from_row(ksnew_ref[0]) + pe) * MLA_SCALE
        m_fin = jnp.maximum(m_new, sc)
        a = jnp.exp(m_new - m_fin)
        pn = jnp.exp(sc - m_fin)
        l_fin = a * l_new + pn
        acc = _lane_tile(a, MLA_KV_LORA // LANES) * acc_new + _bf16_round(pn[:, 0:1]) * cn
        o_lat = acc / _lane_tile(l_fin, MLA_KV_LORA // LANES)
        full = jnp.dot(o_lat.astype(BF16), wuv_ref[...], preferred_element_type=F32)
        r = lax.broadcasted_iota(jnp.int32, full.shape, 0)
        cidx = lax.broadcasted_iota(jnp.int32, full.shape, 1)
        o_ref[0] = jnp.sum(jnp.where(r == cidx // MLA_V, full, 0.0), axis=0, keepdims=True)


def mla_decode(q_lat, q_pe, c_new, kpe_new, ks_new, w_uv, cache_ckv, cache_kpe_t, cache_ks_t, page_table):
    n, nh, cdim = q_lat.shape
    rdim = q_pe.shape[2]
    npages = page_table.shape[1]
    npg = MLA_DEC_PAGES
    assert npages % npg == 0
    per_n3 = lambda i, g, pt: (i, 0, 0)
    hbm = pl.BlockSpec(memory_space=pl.ANY)
    in_specs = [
        pl.BlockSpec((1, nh, cdim), per_n3),
        pl.BlockSpec((1, nh, rdim), per_n3),
        pl.BlockSpec((1, 1, cdim), per_n3),
        pl.BlockSpec((1, 1, rdim), per_n3),
        pl.BlockSpec((1, 1, nh), per_n3),
        pl.BlockSpec(w_uv.shape, lambda i, g, pt: (0, 0)),
        hbm, hbm, hbm,
    ]
    grid_spec = pltpu.PrefetchScalarGridSpec(
        num_scalar_prefetch=1, grid=(n, npages // npg), in_specs=in_specs,
        out_specs=pl.BlockSpec((1, 1, w_uv.shape[1]), per_n3),
        scratch_shapes=[pltpu.VMEM((PAGE_SLOTS, npg, PAGE_SIZE, cdim), F32),
                        pltpu.VMEM((PAGE_SLOTS, npg, rdim, PAGE_SIZE), F32),
                        pltpu.VMEM((PAGE_SLOTS, npg, nh, PAGE_SIZE), F32), pltpu.SemaphoreType.DMA((3, PAGE_SLOTS)),
                        pltpu.VMEM((nh, LANES), F32), pltpu.VMEM((nh, LANES), F32), pltpu.VMEM((nh, cdim), F32)])
    return pl.pallas_call(
        functools.partial(_mla_decode_body, npg=npg),
        grid_spec=grid_spec,
        out_shape=jax.ShapeDtypeStruct((n, 1, w_uv.shape[1]), F32),
        compiler_params=pltpu.CompilerParams(
            dimension_semantics=("arbitrary", "arbitrary"),
            vmem_limit_bytes=V7X_VMEM_LIMIT_BYTES),
        name="mla_decode",
    )(page_table.reshape(-1), q_lat, q_pe, c_new, kpe_new, ks_new, w_uv, cache_ckv, cache_kpe_t, cache_ks_t)


NSA_DEC_PAGES = 16
NSA_DEC_ROWS = 8
NSA_DEC_PICK = NSA_TOPN - NSA_LOCAL - 1
NSA_DEC_NSEL = NSA_TOPN - 1


def _nsa_dec_select_body(pt_ref, q_ref, place_ref, cmp_hbm, oc_ref, idx_ref, parts_ref, page_buf, sems, *, npg, ngrp):
    g = pl.program_id(1)
    slot = _gather_pages_two_slot(pt_ref, ((cmp_hbm, page_buf),), sems, npg)
    bpp = PAGE_SIZE // NSA_BLK
    lane = lax.broadcasted_iota(jnp.int32, (NSA_DH, LANES), 1)
    blks = [jnp.zeros((NSA_DH, LANES), F32) for _ in range(4)]
    for s in range(npg):
        for kv in range(2):
            for h in range(NSA_KV_HEADS):
                slab = page_buf[slot, s, kv, h]
                for b in range(bpp):
                    inblk = (lane >= b * NSA_BLK) & (lane < (b + 1) * NSA_BLK)
                    mean = jnp.sum(jnp.where(inblk, slab, 0.0), axis=1, keepdims=True) * (1.0 / NSA_BLK)
                    t = kv * NSA_KV_HEADS + h
                    blks[t] = jnp.where(lane == s * bpp + b, mean, blks[t])
    for t in range(4):
        parts_ref[g, t] = blks[t]

    @pl.when(g == ngrp - 1)
    def _():
        nb = ngrp * npg * bpp
        nkv = NSA_KV_HEADS
        means_t = [jnp.dot(jnp.concatenate([parts_ref[gg, t] for gg in range(ngrp)], axis=1).astype(BF16),
                           place_ref[...], preferred_element_type=F32).astype(BF16) for t in range(2 * nkv)]
        ck_t = jnp.stack(means_t[:nkv])
        cv_t = jnp.stack(means_t[nkv:])
        q = (q_ref[0] * NSA_SCALE).astype(BF16)
        s_c = jnp.einsum('hqd,hdb->hqb', q, ck_t, preferred_element_type=F32)
        e_c = jnp.exp(s_c - jnp.max(s_c, axis=2, keepdims=True))
        p_c = e_c / jnp.sum(e_c, axis=2, keepdims=True)
        oc_ref[0] = jnp.einsum('hqb,hdb->hqd', p_c.astype(BF16), cv_t, preferred_element_type=F32)
        row = lax.broadcasted_iota(jnp.int32, p_c.shape, 1)
        imp = jnp.sum(jnp.where(row < NSA_GROUP, p_c, 0.0), axis=1)
        jl = lax.broadcasted_iota(jnp.int32, (nb, nb), 1)
        js = lax.broadcasted_iota(jnp.int32, (nb, nb), 0)
        jrow = lax.broadcasted_iota(jnp.int32, (1, nb), 1)
        slot_id = lax.broadcasted_iota(jnp.int32, (nb, LANES), 1)
        jsub = lax.broadcasted_iota(jnp.int32, (nb, LANES), 0)
        slot_row = lax.broadcasted_iota(jnp.int32, (1, LANES), 1)
        for h in range(nkv):
            x_row = jnp.where((jrow >= 1) & (jrow <= nb - NSA_LOCAL), imp[h:h + 1], -1.0)
            x_col = _col_from_row(x_row)
            beaten = (x_row > x_col) | ((x_row == x_col) & (jl < js))
            rank = jnp.sum(beaten.astype(jnp.int32), axis=1, keepdims=True)
            picks = jnp.sum(jnp.where((rank == slot_id) & (slot_id < NSA_DEC_PICK), jsub, 0), axis=0, keepdims=True)
            idx_ref[0, h:h + 1, :] = jnp.where(slot_row == NSA_DEC_PICK + 1, nb - 1, picks)


def nsa_decode_select(q8, cache_cmp_t, page_table):
    n = q8.shape[0]
    npages = page_table.shape[1]
    npg = NSA_DEC_PAGES
    bpp = PAGE_SIZE // NSA_BLK
    assert npages % npg == 0 and npg * bpp <= LANES
    ngrp = npages // npg
    nb = npages * bpp
    assert nb - NSA_LOCAL >= NSA_DEC_PICK + 1
    place = np.zeros((ngrp * LANES, nb), np.float32)
    c = np.arange(npg * bpp)
    for gg in range(ngrp):
        place[gg * LANES + c, gg * npg * bpp + c] = 1.0
    in_specs = [pl.BlockSpec((1, NSA_KV_HEADS, NSA_DEC_ROWS, NSA_DH), lambda i, g, pt: (i, 0, 0, 0)),
                pl.BlockSpec((ngrp * LANES, nb), lambda i, g, pt: (0, 0)),
                pl.BlockSpec(memory_space=pl.ANY)]
    grid_spec = pltpu.PrefetchScalarGridSpec(
        num_scalar_prefetch=1, grid=(n, ngrp), in_specs=in_specs,
        out_specs=[pl.BlockSpec((1, NSA_KV_HEADS, NSA_DEC_ROWS, NSA_DH), lambda i, g, pt: (i, 0, 0, 0)),
                   pl.BlockSpec((1, NSA_KV_HEADS, LANES), lambda i, g, pt: (i, 0, 0))],
        scratch_shapes=[pltpu.VMEM((ngrp, 2 * NSA_KV_HEADS, NSA_DH, LANES), F32),
                        pltpu.VMEM((PAGE_SLOTS, npg, 2, NSA_KV_HEADS, NSA_DH, PAGE_SIZE), F32),
                        pltpu.SemaphoreType.DMA((1, PAGE_SLOTS))])
    return pl.pallas_call(
        functools.partial(_nsa_dec_select_body, npg=npg, ngrp=ngrp),
        grid_spec=grid_spec,
        out_shape=[jax.ShapeDtypeStruct((n, NSA_KV_HEADS, NSA_DEC_ROWS, NSA_DH), F32),
                   jax.ShapeDtypeStruct((n, NSA_KV_HEADS, LANES), jnp.int32)],
        compiler_params=pltpu.CompilerParams(
            dimension_semantics=("arbitrary", "arbitrary"),
            vmem_limit_bytes=V7X_VMEM_LIMIT_BYTES),
        name="nsa_decode_select",
    )(page_table.reshape(-1), q8, jnp.asarray(place, BF16), cache_cmp_t)


def _nsa_dec_attend_body(pg_ref, half_ref, q_ref, gate_ref, oc_ref, new_ref, *refs, nsel):
    sel_refs = refs[:nsel]
    win_ref, o_ref, wout_ref = refs[nsel:]
    i, h = pl.program_id(0), pl.program_id(1)
    qf = _bf16_round(q_ref[0, 0] * NSA_SCALE)
    q = qf.astype(BF16)
    new = new_ref[0, 0]
    lane = lax.broadcasted_iota(jnp.int32, (NSA_DEC_ROWS, PAGE_SIZE), 1)

    def attend(score_tiles, value_tiles, k_new, v_new):
        s_new = jnp.sum(qf * _bf16_round(k_new), axis=1, keepdims=True)
        sall = jnp.concatenate(score_tiles, axis=1) if len(score_tiles) > 1 else score_tiles[0]
        m = jnp.maximum(jnp.max(sall, axis=1, keepdims=True), s_new)
        p = jnp.exp(sall - m)
        p_new = jnp.exp(s_new - m)
        l = jnp.sum(p, axis=1, keepdims=True) + p_new
        pb = p.astype(BF16)
        acc = _bf16_round(p_new) * _bf16_round(v_new)
        off = 0
        for vt in value_tiles:
            w = vt.shape[1]
            acc = acc + _dot_nt(pb[:, off:off + w], vt)
            off += w
        return acc / l

    s_tiles, v_tiles = [], []
    for b in range(nsel):
        hb = half_ref[(i * NSA_KV_HEADS + h) * nsel + b]
        sb = jnp.dot(q, sel_refs[b][0, 0, 0].astype(BF16), preferred_element_type=F32)
        s_tiles.append(jnp.where(lax.shift_right_logical(lane, 6) == hb, sb, NEG_BIG))
        v_tiles.append(sel_refs[b][0, 1, 0].astype(BF16))
    o_s = attend(s_tiles, v_tiles, new[0:1], new[1:2])
    s_w = jnp.dot(q, win_ref[0, 0, 0].astype(BF16), preferred_element_type=F32)
    o_w = attend([s_w], [win_ref[0, 1, 0].astype(BF16)], new[2:3], new[3:4])
    gt = gate_ref[0, 0]
    o_ref[0, 0] = gt[:, 0:1] * oc_ref[0, 0] + gt[:, 1:2] * o_s + gt[:, 2:3] * o_w

    wl = win_ref.shape[4]
    wlane = lax.broadcasted_iota(jnp.int32, (NSA_DH, wl), 1)
    for kv in range(2):
        col = _col_from_row(new[2 + kv:3 + kv])
        wout_ref[0, kv, 0] = jnp.where(wlane == wl - 1, col, pltpu.roll(win_ref[0, kv, 0], wl - 1, axis=1))


def nsa_decode_attend(pages, halves, q8, gates8, oc8, new_rows, cache_sel_t, win_t):
    n = q8.shape[0]
    nsel = NSA_DEC_NSEL
    wl = win_t.shape[4]
    per = lambda i, h, pg, hf: (i, h, 0, 0)

    def sel_map(b):
        return lambda i, h, pg, hf: (pg[(i * NSA_KV_HEADS + h) * nsel + b], 0, h, 0, 0)

    in_specs = [pl.BlockSpec((1, 1, NSA_DEC_ROWS, NSA_DH), per),
                pl.BlockSpec((1, 1, NSA_DEC_ROWS, 3), per),
                pl.BlockSpec((1, 1, NSA_DEC_ROWS, NSA_DH), per),
                pl.BlockSpec((1, 1, 4, NSA_DH), per)]
    in_specs += [pl.BlockSpec((1, 2, 1, NSA_DH, PAGE_SIZE), sel_map(b)) for b in range(nsel)]
    in_specs += [pl.BlockSpec((1, 2, 1, NSA_DH, wl), lambda i, h, pg, hf: (i, 0, h, 0, 0))]
    grid_spec = pltpu.PrefetchScalarGridSpec(
        num_scalar_prefetch=2, grid=(n, NSA_KV_HEADS), in_specs=in_specs,
        out_specs=[pl.BlockSpec((1, 1, NSA_DEC_ROWS, NSA_DH), per),
                   pl.BlockSpec((1, 2, 1, NSA_DH, wl), lambda i, h, pg, hf: (i, 0, h, 0, 0))])
    return pl.pallas_call(
        functools.partial(_nsa_dec_attend_body, nsel=nsel),
        grid_spec=grid_spec,
        out_shape=[jax.ShapeDtypeStruct((n, NSA_KV_HEADS, NSA_DEC_ROWS, NSA_DH), F32),
                   jax.ShapeDtypeStruct(win_t.shape, F32)],
        compiler_params=pltpu.CompilerParams(
            dimension_semantics=("parallel", "arbitrary"),
            vmem_limit_bytes=V7X_VMEM_LIMIT_BYTES),
        name="nsa_decode_attend",
    )(pages, halves, q8, gates8, oc8, new_rows, *([cache_sel_t] * nsel), win_t)


def nsa_sample_pallas(q, kv, gates, cache_cmp, cache_sel, win_state, page_table):
    n = q.shape[0]
    assert q.shape[3] == 1 and win_state.shape[3] == NSA_WINDOW
    bpp = PAGE_SIZE // NSA_BLK
    padrows = ((0, 0), (0, 0), (0, NSA_DEC_ROWS - NSA_GROUP), (0, 0))
    q8 = jnp.pad(q[:, :, :, 0], padrows)
    gates8 = jnp.pad(gates[:, :, :, 0], padrows)
    feature_major = (0, 1, 2, 4, 3)
    oc8, picks = nsa_decode_select(q8, jnp.transpose(cache_cmp, feature_major), page_table)
    idx = picks[:, :, :NSA_DEC_NSEL]
    pages = jnp.take_along_axis(page_table[:, None, :], idx // bpp, axis=2)
    new_rows = jnp.stack([kv[:, 1, 0, :, 0], kv[:, 1, 1, :, 0], kv[:, 2, 0, :, 0], kv[:, 2, 1, :, 0]], axis=2)
    o8, win_t = nsa_decode_attend(pages.reshape(-1), (idx % bpp).reshape(-1), q8, gates8, oc8, new_rows,
                                  jnp.transpose(cache_sel, feature_major), jnp.transpose(win_state, feature_major))
    o = o8[:, :, :NSA_GROUP].reshape(n, 1, NSA_HEADS * NSA_DH)
    return o, jnp.transpose(win_t, feature_major)


def split_cols(x, sizes):
    return jnp.split(x, np.cumsum(sizes)[:-1].tolist(), axis=-1)


def rms_norm(x, g):
    xf = x.astype(F32)
    y = xf * lax.rsqrt(jnp.mean(xf * xf, axis=-1, keepdims=True) + EPS)
    return (y * g.astype(F32)).astype(x.dtype)


def layer_norm(x, g, b):
    xf = x.astype(F32)
    xc = xf - jnp.mean(xf, axis=-1, keepdims=True)
    y = xc * lax.rsqrt(jnp.mean(xc * xc, axis=-1, keepdims=True) + EPS)
    return (y * g.astype(F32) + b.astype(F32)).astype(x.dtype)


def l2_normalize(x):
    xf = x.astype(F32)
    return xf * lax.rsqrt(jnp.sum(xf * xf, axis=-1, keepdims=True) + EPS)


def rope(x, pos):
    d = x.shape[-1]
    inv = ROPE_THETA ** (-jnp.arange(0, d, 2, dtype=F32) / d)
    ang = pos.astype(F32)[..., None] * inv
    cos, sin = jnp.cos(ang), jnp.sin(ang)
    xf = x.astype(F32)
    x1, x2 = xf[..., : d // 2], xf[..., d // 2:]
    return jnp.concatenate([x1 * cos - x2 * sin, x2 * cos + x1 * sin], axis=-1).astype(x.dtype)


def partial_rope(x, pos, rot):
    return jnp.concatenate([rope(x[..., :rot], pos), x[..., rot:]], axis=-1)


def causal_conv(x_ext, w):
    t = x_ext.shape[1] - (GDN_CONV - 1)
    return sum(x_ext[:, i:i + t] * w[i] for i in range(GDN_CONV))


def gdn_prepare(conv_out, b_raw, a_raw, a_log, dt_bias):
    n, t, _ = conv_out.shape
    q, k, v = split_cols(conv_out, (GDN_HEADS * GDN_DK, GDN_HEADS * GDN_DK, GDN_HEADS * GDN_DV))
    q = l2_normalize(q.reshape(n, t, GDN_HEADS, GDN_DK)) * (GDN_DK ** -0.5)
    k = l2_normalize(k.reshape(n, t, GDN_HEADS, GDN_DK))
    v = v.reshape(n, t, GDN_HEADS, GDN_DV).astype(F32)
    beta = jax.nn.sigmoid(b_raw.astype(F32))
    g = -jnp.exp(a_log.astype(F32)) * jax.nn.softplus(a_raw.astype(F32) + dt_bias.astype(F32))
    return q, k, v, beta, g


GDN_PREP_ROWS = 256


def _gdn_prep_body(x_ref, halo_ref, ba_ref, w_ref, alog_ref, dtb_ref, q_ref, k_ref, kt_ref, v_ref, b_ref, g_ref):
    rows = x_ref.shape[0]
    nh = GDN_HEADS
    first = pl.program_id(0) == 0
    halo = jnp.where(first, 0.0, halo_ref[...])
    ext = jnp.concatenate([halo, x_ref[...]], axis=0)
    w = w_ref[...]
    conv = ext[8:8 + rows] * w[GDN_CONV - 1:GDN_CONV]
    for i in range(GDN_CONV - 1):
        off = 8 - (GDN_CONV - 1) + i
        conv = conv + ext[off:off + rows] * w[i:i + 1]
    act = conv / (1.0 + jnp.exp(-conv))
    qw = nh * GDN_DK
    for h in range(nh):
        q = act[:, h * GDN_DK:(h + 1) * GDN_DK]
        k = act[:, qw + h * GDN_DK:qw + (h + 1) * GDN_DK]
        q_ref[h] = q * lax.rsqrt(jnp.sum(q * q, axis=1, keepdims=True) + EPS) * (GDN_DK ** -0.5)
        kn = k * lax.rsqrt(jnp.sum(k * k, axis=1, keepdims=True) + EPS)
        k_ref[h] = kn
        kt_ref[h] = kn.T
        v_ref[h] = act[:, 2 * qw + h * GDN_DV:2 * qw + (h + 1) * GDN_DV]
    ba = ba_ref[...]
    b_ref[...] = 1.0 / (1.0 + jnp.exp(-ba[:, 0:nh]))
    z = ba[:, nh:2 * nh] + dtb_ref[...]
    softplus = jnp.maximum(z, 0.0) + jnp.log(1.0 + jnp.exp(-jnp.abs(z)))
    g_ref[...] = -jnp.exp(alog_ref[...]) * softplus


def gdn_prep(proj, conv_w, a_log, dt_bias):
    t = proj.shape[0]
    rows = GDN_PREP_ROWS
    nh, cw = GDN_HEADS, GDN_CONV_CH
    ba_col = (cw + nh * GDN_DV) // LANES
    assert t % rows == 0 and cw % LANES == 0 and (cw + nh * GDN_DV) % LANES == 0 and 2 * nh <= LANES
    hrow = lambda i: (0, i, 0)
    sds = jax.ShapeDtypeStruct
    return pl.pallas_call(
        _gdn_prep_body,
        grid=(t // rows,),
        in_specs=[pl.BlockSpec((rows, cw), lambda i: (i, 0)),
                  pl.BlockSpec((8, cw), lambda i: (jnp.maximum(i * (rows // 8) - 1, 0), 0)),
                  pl.BlockSpec((rows, LANES), lambda i: (i, ba_col)),
                  pl.BlockSpec((GDN_CONV, cw), lambda i: (0, 0)),
                  pl.BlockSpec((1, nh), lambda i: (0, 0)), pl.BlockSpec((1, nh), lambda i: (0, 0))],
        out_specs=[pl.BlockSpec((nh, rows, GDN_DK), hrow), pl.BlockSpec((nh, rows, GDN_DK), hrow),
                   pl.BlockSpec((nh, GDN_DK, rows), lambda i: (0, 0, i)), pl.BlockSpec((nh, rows, GDN_DV), hrow),
                   pl.BlockSpec((rows, nh), lambda i: (i, 0)), pl.BlockSpec((rows, nh), lambda i: (i, 0))],
        out_shape=[sds((nh, t, GDN_DK), F32), sds((nh, t, GDN_DK), F32), sds((nh, GDN_DK, t), F32),
                   sds((nh, t, GDN_DV), F32), sds((t, nh), F32), sds((t, nh), F32)],
        compiler_params=pltpu.CompilerParams(
            dimension_semantics=("parallel",),
            vmem_limit_bytes=V7X_VMEM_LIMIT_BYTES),
        name="gdn_prep",
    )(proj, proj, proj, conv_w, a_log.reshape(1, nh), dt_bias.reshape(1, nh))


GDN_SUPER = 256


def _bmm(a, b):
    return jnp.einsum('hij,hjk->hik', a, b, preferred_element_type=F32)


def _gdn_chunk_body(q_ref, k_ref, kt_ref, v_ref, b_ref, g_ref, o_ref, sfin_ref, s_ref, vnew_ref):
    n, c, nh = GDN_SUPER, GDN_CHUNK, GDN_HEADS

    @pl.when(pl.program_id(0) == 0)
    def _():
        s_ref[...] = jnp.zeros(s_ref.shape, F32)
        vnew_ref[...] = jnp.zeros(vnew_ref.shape, BF16)

    r = lax.broadcasted_iota(jnp.int32, (n, n), 0)
    cc = lax.broadcasted_iota(jnp.int32, (n, n), 1)
    same = lax.shift_right_logical(r, 6) == lax.shift_right_logical(cc, 6)
    tril = same & (cc <= r)
    stril = same & (cc < r)
    last = same & ((cc & (c - 1)) == c - 1)
    eye = (cc == r).astype(F32)
    gc_all = jnp.dot(tril.astype(F32), g_ref[...], precision=lax.Precision.HIGHEST, preferred_element_type=F32)
    b_all = b_ref[...]
    q, k, v = q_ref[...], k_ref[...], v_ref[...]
    ms, decays, vbs, kbgs, qgs, kdts, gls = [], [], [], [], [], [], []
    for h in range(nh):
        gc = gc_all[:, h:h + 1]
        beta = b_all[:, h:h + 1]
        gc_row = jnp.sum(jnp.where(cc == r, jnp.broadcast_to(gc, (n, n)), 0.0), axis=0, keepdims=True)
        glast_col = jnp.sum(jnp.where(last, jnp.broadcast_to(gc_row, (n, n)), 0.0), axis=1, keepdims=True)
        glast_row = jnp.sum(jnp.where(cc == r, jnp.broadcast_to(glast_col, (n, n)), 0.0), axis=0, keepdims=True)
        decays.append(jnp.where(tril, jnp.exp(jnp.where(tril, gc - gc_row, 0.0)), 0.0))
        eg = jnp.exp(gc)
        kb = k[h] * beta
        ms.append(kb.astype(BF16))
        vbs.append((v[h] * beta).astype(BF16))
        kbgs.append((kb * eg).astype(BF16))
        qgs.append((q[h] * eg).astype(BF16))
        kdts.append((kt_ref[h] * jnp.exp(glast_row - gc_row)).astype(BF16))
        gls.append(jnp.exp(glast_col))
    decay = jnp.stack(decays)
    kbf = k.astype(BF16)
    kkt = jnp.einsum('hid,hjd->hij', jnp.stack(ms), kbf, preferred_element_type=F32)
    m = jnp.where(stril[None], kkt * decay, 0.0)
    attn = jnp.where(tril[None], jnp.einsum('hid,hjd->hij', q.astype(BF16), kbf, preferred_element_type=F32) * decay,
                     0.0).astype(BF16)
    tinv = eye[None] - m
    pw = m
    for _ in range(int(np.log2(c)) - 1):
        pwb = pw.astype(BF16)
        pw = _bmm(pwb, pwb)
        tinv = tinv + _bmm(tinv.astype(BF16), pw.astype(BF16))
    tb = tinv.astype(BF16)
    u = _bmm(tb, jnp.stack(vbs))
    w = _bmm(tb, jnp.stack(kbgs)).astype(BF16)
    qg = jnp.stack(qgs)
    kdt = jnp.stack(kdts)
    gl = jnp.stack(gls)
    state = s_ref[...]
    rown = lax.broadcasted_iota(jnp.int32, (n, GDN_DV), 0)
    for ci in range(n // c):
        rows = slice(ci * c, (ci + 1) * c)
        sb = state.astype(BF16)
        vb = (u[:, rows] - _bmm(w[:, rows], sb)).astype(BF16)
        vnew_ref[:, rows, :] = vb
        vall = vnew_ref[...]
        o_ref[:, rows, :] = _bmm(qg[:, rows], sb) + _bmm(attn[:, rows], vall)
        vonly = jnp.where((lax.shift_right_logical(rown, 6) == ci)[None], vall, jnp.zeros_like(vall))
        state = state * gl[:, ci * c:ci * c + 1] + _bmm(kdt, vonly)
    s_ref[...] = state
    sfin_ref[...] = state


def gdn_chunked_pallas(q, k, kt, v, beta, g):
    nh, t, _ = q.shape
    n = GDN_SUPER
    assert t % n == 0 and GDN_DK == GDN_DV and n % GDN_CHUNK == 0 and GDN_CHUNK == 64
    hrow = lambda i: (0, i, 0)
    return pl.pallas_call(
        _gdn_chunk_body,
        grid=(t // n,),
        in_specs=[pl.BlockSpec((nh, n, GDN_DK), hrow), pl.BlockSpec((nh, n, GDN_DK), hrow),
                  pl.BlockSpec((nh, GDN_DK, n), lambda i: (0, 0, i)), pl.BlockSpec((nh, n, GDN_DV), hrow),
                  pl.BlockSpec((n, nh), lambda i: (i, 0)), pl.BlockSpec((n, nh), lambda i: (i, 0))],
        out_specs=[pl.BlockSpec((nh, n, GDN_DV), hrow), pl.BlockSpec((nh, GDN_DK, GDN_DV), lambda i: (0, 0, 0))],
        out_shape=[jax.ShapeDtypeStruct((nh, t, GDN_DV), F32), jax.ShapeDtypeStruct((nh, GDN_DK, GDN_DV), F32)],
        scratch_shapes=[pltpu.VMEM((nh, GDN_DK, GDN_DV), F32), pltpu.VMEM((nh, n, GDN_DV), BF16)],
        compiler_params=pltpu.CompilerParams(
            dimension_semantics=("arbitrary",),
            vmem_limit_bytes=V7X_VMEM_LIMIT_BYTES),
        name="gdn_chunked",
    )(q, k, kt, v, beta, g)


def gdn_recurrent(q, k, v, beta, g, s0):
    def step(s, xs):
        q_t, k_t, v_t, b_t, g_t = xs
        s = s * jnp.exp(g_t)[..., None, None]
        delta = (v_t - jnp.einsum('nhde,nhd->nhe', s, k_t)) * b_t[..., None]
        s = s + jnp.einsum('nhd,nhe->nhde', k_t, delta)
        return s, jnp.einsum('nhde,nhd->nhe', s, q_t)

    xs = tuple(jnp.moveaxis(x, 1, 0) for x in (q, k, v, beta, g))
    s_fin, o = lax.scan(step, s0, xs)
    return jnp.moveaxis(o, 0, 1), s_fin


def gdn_output(o, z, out_norm):
    n, t = z.shape[:2]
    gate = jax.nn.silu(z.reshape(n, t, GDN_HEADS, GDN_DV).astype(F32))
    return (rms_norm(o, out_norm) * gate).astype(z.dtype).reshape(n, t, GDN_HEADS * GDN_DV)


def even_project(xn, pos, w_in, q_norm, k_norm, proj=None):
    n, t, _ = xn.shape
    qkv, z, b_raw, a_raw, q, kv, g_raw = split_cols(xn @ w_in if proj is None else proj, EVEN_SIZES)
    q = partial_rope(rms_norm(q.reshape(n, t, NSA_HEADS, NSA_DH), q_norm), pos[:, None], NSA_ROT)
    q = jnp.transpose(q.reshape(n, t, NSA_KV_HEADS, NSA_GROUP, NSA_DH), (0, 2, 3, 1, 4))
    kv = kv.reshape(n, t, 3, 2, NSA_KV_HEADS, NSA_DH)
    k = partial_rope(rms_norm(kv[:, :, :, 0], k_norm[:, None, :]), pos[:, None, None], NSA_ROT)
    kv = jnp.transpose(jnp.stack([k, kv[:, :, :, 1]], axis=3), (0, 2, 3, 4, 1, 5))
    gates = jax.nn.sigmoid(g_raw.astype(F32)).reshape(n, t, NSA_KV_HEADS, NSA_GROUP, 3)
    return qkv, z, b_raw, a_raw, q, kv, jnp.transpose(gates, (0, 2, 3, 1, 4))


def even_prompt(xn, pos, w_in, w_out, conv_w, a_log, dt_bias, out_norm, q_norm, k_norm):
    n, t, _ = xn.shape
    assert n == 1 and t >= GDN_CONV - 1
    proj = xn @ w_in
    qkv, z, _, _, q, kv, gates = even_project(xn, pos, w_in, q_norm, k_norm, proj=proj)
    qh, kh, kth, vh, beta, g = gdn_prep(proj[0], conv_w, a_log, dt_bias)
    o_h, s_fin = gdn_chunked_pallas(qh, kh, kth, vh, beta, g)
    o, s_fin = jnp.transpose(o_h, (1, 0, 2))[None], s_fin[None]
    nsa_out, win = nsa_prompt_pallas(q, kv, gates)
    y = jnp.concatenate([gdn_output(o, z, out_norm), nsa_out], axis=-1) @ w_out
    return y, (s_fin.astype(xn.dtype), qkv[:, -(GDN_CONV - 1):], kv[:, 0], kv[:, 1], win)


def even_sample(xn, pos, gdn_state, conv_state, cache_cmp, cache_sel, win_state, page_table,
                w_in, w_out, conv_w, a_log, dt_bias, out_norm, q_norm, k_norm):
    qkv, z, b_raw, a_raw, q, kv, gates = even_project(xn, pos, w_in, q_norm, k_norm)
    qkv_ext = jnp.concatenate([conv_state.astype(qkv.dtype), qkv], axis=1)
    gq, gk, gv, beta, g = gdn_prepare(jax.nn.silu(causal_conv(qkv_ext, conv_w)), b_raw, a_raw, a_log, dt_bias)
    o, s_fin = gdn_recurrent(gq, gk, gv, beta, g, gdn_state.astype(F32))
    nsa_out, win = nsa_sample_pallas(q, kv, gates, cache_cmp, cache_sel, win_state, page_table)
    y = jnp.concatenate([gdn_output(o, z, out_norm), nsa_out], axis=-1) @ w_out
    return y, (s_fin.astype(xn.dtype), qkv_ext[:, -(GDN_CONV - 1):], kv[:, 0], kv[:, 1], win)


def sgu_mix(u, v, w_s, b_s):
    n, t, _ = u.shape
    l = min(t, SGU_CHUNK)
    nc = t // l
    w = jnp.tril(w_s[:, :l, :l])
    mix = jnp.einsum('gts,ncsgd->nctgd', w, v.reshape(n, nc, l, SGU_GROUPS, SGU_DG))
    mix = mix + jnp.transpose(b_s[:, :l])[:, :, None]
    return (u.reshape(n, nc, l, SGU_GROUPS, SGU_DG) * mix).reshape(n, t, SGU_WIDTH)


def odd_project(xn, pos, w_in, ln_g, ln_b, cq_norm, ckv_norm, w_uq, w_uk, qn_norm, qr_norm, kr_norm):
    n, t, _ = xn.shape
    uv, cq, ckv, kr = split_cols(xn @ w_in, ODD_SIZES)
    uv = jax.nn.gelu(uv, approximate=False)
    u = uv[..., :SGU_WIDTH]
    v = layer_norm(uv[..., SGU_WIDTH:], ln_g, ln_b)
    q = (rms_norm(cq, cq_norm) @ w_uq).reshape(n, t, MLA_HEADS, MLA_NOPE + MLA_ROPE)
    q_nope = rms_norm(q[..., :MLA_NOPE], qn_norm)
    q_pe = rope(rms_norm(q[..., MLA_NOPE:], qr_norm), pos[:, None])
    c = rms_norm(ckv, ckv_norm)
    k_pe = rope(rms_norm(kr, kr_norm), pos)
    k_nope = jnp.einsum('ntc,chd->nthd', c, w_uk).astype(F32)
    kscale = lax.rsqrt(jnp.mean(k_nope * k_nope, axis=-1) + EPS)
    return u, v, q_nope, q_pe, c, k_pe, k_nope, kscale


def odd_prompt(xn, pos, w_in, w_out, ln_g, ln_b, sgu_w, sgu_b, cq_norm, ckv_norm, w_uq, w_uk, w_uv,
               qn_norm, qr_norm, kn_norm, kr_norm):
    n, t, _ = xn.shape
    u, v, q_nope, q_pe, c, k_pe, k_nope, kscale = odd_project(
        xn, pos, w_in, ln_g, ln_b, cq_norm, ckv_norm, w_uq, w_uk, qn_norm, qr_norm, kr_norm)
    sgu = sgu_mix(u, v, sgu_w, sgu_b)
    k_nope_n = (k_nope * kscale[..., None] * kn_norm.astype(F32)).astype(xn.dtype)
    pad = jnp.zeros((n, t, MLA_HEADS, LANES - MLA_NOPE - MLA_ROPE), F32)
    q = jnp.concatenate([q_nope * MLA_SCALE, q_pe * MLA_SCALE, pad], axis=-1)
    k = jnp.concatenate([k_nope_n, jnp.broadcast_to(k_pe[:, :, None, :], (n, t, MLA_HEADS, MLA_ROPE)), pad], axis=-1)
    vv = jnp.einsum('ntc,chd->nthd', c, w_uv)
    assert n == 1
    att = flash_causal_pairs(q.reshape(t, MLA_HEADS * LANES).astype(BF16),
                             k.reshape(t, MLA_HEADS * LANES).astype(BF16),
                             vv.reshape(t, MLA_HEADS * MLA_V).astype(BF16)).reshape(n, t, MLA_HEADS * MLA_V)
    y = jnp.concatenate([sgu, att], axis=-1) @ w_out
    return y, (c, k_pe, kscale.astype(xn.dtype))


def odd_sample(xn, pos, cache_ckv, cache_kpe, cache_kscale, page_table, w_in, w_out, ln_g, ln_b, sgu_w,
               sgu_b, cq_norm, ckv_norm, w_uq, w_uk, w_uv, qn_norm, qr_norm, kn_norm, kr_norm):
    n, s, _ = xn.shape
    u, v, q_nope, q_pe, c, k_pe, k_nope, kscale = odd_project(
        xn, pos, w_in, ln_g, ln_b, cq_norm, ckv_norm, w_uq, w_uk, qn_norm, qr_norm, kr_norm)
    sgu = sgu_mix(u, v, sgu_w, sgu_b)
    q_lat = jnp.einsum('nqhd,chd->nqhc', q_nope.astype(F32) * kn_norm.astype(F32), w_uk)
    assert s == 1
    att = mla_decode(q_lat[:, 0], q_pe[:, 0], c, k_pe, kscale.astype(F32),
                     w_uv.reshape(MLA_KV_LORA, MLA_HEADS * MLA_V).astype(BF16), cache_ckv,
                     jnp.swapaxes(cache_kpe, 1, 2), jnp.swapaxes(cache_kscale, 1, 2), page_table)
    y = jnp.concatenate([sgu, att], axis=-1) @ w_out
    return y, (c, k_pe, kscale.astype(xn.dtype), v)


def kernel(x_prompt, x_sample, state_gdn, state_gdn_conv, cache_nsa_cmp, cache_nsa_sel, state_nsa_win,
           cache_mla_ckv, cache_mla_kpe, cache_mla_kscale, page_table, norm_mix, norm_ffn, ffn_up, ffn_down,
           w_in_even, w_out_even, gdn_conv_w, gdn_a_log, gdn_dt_bias, gdn_out_norm, nsa_q_norm, nsa_k_norm,
           w_in_odd, w_out_odd, sgu_ln_g, sgu_ln_b, sgu_w, sgu_b, mla_cq_norm, mla_ckv_norm, mla_w_uq,
           mla_w_uk, mla_w_uv, mla_qn_norm, mla_qr_norm, mla_kn_norm, mla_kr_norm):
    pos_p = jnp.arange(x_prompt.shape[1], dtype=jnp.int32)
    pos_s = page_table.shape[1] * PAGE_SIZE + jnp.arange(x_sample.shape[1], dtype=jnp.int32)
    ffn_up_b = ffn_up.astype(BF16)
    ffn_down_b = ffn_down.astype(BF16)
    hp, hs = x_prompt, x_sample
    for layer in range(norm_mix.shape[0]):
        xp = rms_norm(hp, norm_mix[layer])
        xs = rms_norm(hs, norm_mix[layer])
        if layer % 2 == 0:
            mp, (p_gdn_state, p_gdn_conv, p_nsa_cmp, p_nsa_sel, p_nsa_win) = even_prompt(
                xp, pos_p, w_in_even, w_out_even, gdn_conv_w, gdn_a_log, gdn_dt_bias, gdn_out_norm,
                nsa_q_norm, nsa_k_norm)
            ms, (s_gdn_state, s_gdn_conv, s_nsa_cmp, s_nsa_sel, s_nsa_win) = even_sample(
                xs, pos_s, state_gdn, state_gdn_conv, cache_nsa_cmp, cache_nsa_sel, state_nsa_win, page_table,
                w_in_even, w_out_even, gdn_conv_w, gdn_a_log, gdn_dt_bias, gdn_out_norm, nsa_q_norm, nsa_k_norm)
        else:
            mp, (p_mla_ckv, p_mla_kpe, p_mla_kscale) = odd_prompt(
                xp, pos_p, w_in_odd, w_out_odd, sgu_ln_g, sgu_ln_b, sgu_w, sgu_b, mla_cq_norm, mla_ckv_norm,
                mla_w_uq, mla_w_uk, mla_w_uv, mla_qn_norm, mla_qr_norm, mla_kn_norm, mla_kr_norm)
            ms, (s_mla_ckv, s_mla_kpe, s_mla_kscale, s_sgu_v) = odd_sample(
                xs, pos_s, cache_mla_ckv, cache_mla_kpe, cache_mla_kscale, page_table, w_in_odd, w_out_odd,
                sgu_ln_g, sgu_ln_b, sgu_w, sgu_b, mla_cq_norm, mla_ckv_norm, mla_w_uq, mla_w_uk, mla_w_uv,
                mla_qn_norm, mla_qr_norm, mla_kn_norm, mla_kr_norm)
        hp = sq_relu_mlp(hp + mp, norm_ffn[layer], ffn_up_b[layer], ffn_down_b[layer])
        hs = sq_relu_mlp(hs + ms, norm_ffn[layer], ffn_up_b[layer], ffn_down_b[layer])
    return (hp, hs, p_gdn_state, p_gdn_conv, p_nsa_cmp, p_nsa_sel, p_nsa_win, p_mla_ckv, p_mla_kpe, p_mla_kscale,
            s_gdn_state, s_gdn_conv, s_nsa_cmp, s_nsa_sel, s_nsa_win, s_mla_ckv, s_mla_kpe, s_mla_kscale, s_sgu_v)
```
